```python
import jax
import jax.numpy as jnp
from jax import lax
import numpy as np

D_MODEL = 1024
BATCH = 8
SEQ = 2048
DEPTH = 4

N_MIXERS = 3
RMS_EPS = 1e-6

CHUNK = 128
A_WIDTH = D_MODEL
A_GROUPS = 8
A_GROUP_DIM = A_WIDTH // A_GROUPS
LN_EPS = 1e-5

B_WIDTH = D_MODEL
POOL_WINDOWS = (2, 4, 8, 16)
B_GROUPS = len(POOL_WINDOWS)
B_GROUP_DIM = B_WIDTH // B_GROUPS

C_HEAD_DIM = 64
C_HEADS = D_MODEL // C_HEAD_DIM
DECAY_LORA = 64
AAA_LORA = 64
GATE_LORA = 128
GN_EPS = 64e-5
N_SHIFT = 6

D_FF = D_MODEL * 7 // 2
N_EXPERTS = 8
TOP_K = 2
MOE_BLOCK = 128

N_A_LAYERS = (DEPTH + 2) // 3
N_B_LAYERS = (DEPTH + 1) // 3
N_C_LAYERS = DEPTH // 3
N_DENSE_LAYERS = (DEPTH + 1) // 2
N_MOE_LAYERS = DEPTH // 2

kernel_name = "hybrid_gmlp_pool_rwkv7_moe_trunk"


def rmsnorm(x, g):
    xf = x.astype(jnp.float32)
    y = xf * lax.rsqrt(jnp.mean(xf * xf, axis=-1, keepdims=True) + RMS_EPS)
    return (y * g).astype(x.dtype)


def layernorm(x, g, b):
    xf = x.astype(jnp.float32)
    mu = jnp.mean(xf, axis=-1, keepdims=True)
    var = jnp.mean(jnp.square(xf - mu), axis=-1, keepdims=True)
    return ((xf - mu) * lax.rsqrt(var + LN_EPS) * g + b).astype(x.dtype)


def swiglu(x, w_gate, w_up, w_down):
    return (jax.nn.silu(x @ w_gate) * (x @ w_up)) @ w_down


def chunked_gmlp(h, w_in, ln_g, ln_b, w_s, b_s, w_out):
    B, S, _ = h.shape
    z = jax.nn.gelu(h @ w_in, approximate=False)
    u, v = jnp.split(z, 2, axis=-1)
    v = layernorm(v, ln_g, ln_b)
    v = v.reshape(B, S // CHUNK, CHUNK, A_GROUPS, A_GROUP_DIM)
    causal = jnp.tril(jnp.ones((CHUNK, CHUNK), dtype=bool))
    w = jnp.where(causal, w_s, 0).astype(v.dtype)
    s = jnp.einsum('gts,bcsgd->bctgd', w, v) + b_s.T[:, :, None].astype(v.dtype)
    return (u * s.reshape(B, S, A_WIDTH)) @ w_out


def pooling_mixer(h, w_in, w_grp, scale, w_out):
    B, S, _ = h.shape
    p = (h @ w_in).astype(jnp.float32)
    csum = jnp.cumsum(p, axis=1)
    pos = jnp.arange(1, S + 1)
    diffs = []
    for gi, win in enumerate(POOL_WINDOWS):
        lo, hi = gi * B_GROUP_DIM, (gi + 1) * B_GROUP_DIM
        c = csum[..., lo:hi]
        lag = jnp.pad(c, ((0, 0), (win, 0), (0, 0)))[:, :S]
        count = jnp.minimum(pos, win).astype(jnp.float32)[None, :, None]
        diffs.append((c - lag) / count - p[..., lo:hi])
    d = jnp.stack(diffs, axis=2).astype(h.dtype)
    y = jnp.einsum('bsgi,gio->bsgo', d, w_grp).reshape(B, S, B_WIDTH) * scale
    return y @ w_out


def rwkv7_time_mix(h, mu, w_rkv, w0, w1, w2, a0, a1, a2, g1, g2, k_k, k_a, r_k, ln_g, ln_b, w_o):
    B, S, D = h.shape
    f32 = jnp.float32
    xx = jnp.pad(h, ((0, 0), (1, 0), (0, 0)))[:, :S] - h
    xs = h[None] + xx[None] * mu[:, None, None, :]
    r, k, v = jnp.einsum('nbsd,nde->nbse', xs[:3], w_rkv)
    xw, xa, xg = xs[3], xs[4], xs[5]
    w = -jax.nn.softplus(-(w0 + jnp.tanh(xw @ w1) @ w2)) - 0.5
    a = jax.nn.sigmoid(a0 + (xa @ a1) @ a2)
    g = jax.nn.sigmoid(xg @ g1) @ g2

    def heads(t):
        return t.reshape(B, S, C_HEADS, C_HEAD_DIM).astype(f32)

    kk = heads(k * k_k)
    kk = kk / jnp.maximum(jnp.linalg.norm(kk, axis=-1, keepdims=True), 1e-12)
    k = k * (1 + (a - 1) * k_a)
    r_h, k_h, v_h, a_h = heads(r), heads(k), heads(v), heads(a)
    decay = jnp.exp(-jnp.exp(heads(w)))

    def tm(t):
        return jnp.transpose(t, (1, 0, 2, 3))

    def step(state, inp):
        r_t, w_t, k_t, v_t, aa_t, bb_t = inp
        sa = jnp.einsum('bhij,bhj->bhi', state, aa_t)
        state = (state * w_t[:, :, None, :]
                 + sa[..., :, None] * bb_t[..., None, :]
                 + v_t[..., :, None] * k_t[..., None, :])
        return state, jnp.einsum('bhij,bhj->bhi', state, r_t)

    state0 = jnp.zeros((B, C_HEADS, C_HEAD_DIM, C_HEAD_DIM), f32)
    _, y = lax.scan(step, state0, (tm(r_h), tm(decay), tm(k_h), tm(v_h), tm(-kk), tm(kk * a_h)))
    y = tm(y)
    mean = jnp.mean(y, axis=-1, keepdims=True)
    var = jnp.mean(jnp.square(y - mean), axis=-1, keepdims=True)
    yn = ((y - mean) * lax.rsqrt(var + GN_EPS)).reshape(B, S, D) * ln_g + ln_b
    bonus = jnp.sum(r_h * k_h * r_k, axis=-1, keepdims=True) * v_h
    out = (yn + bonus.reshape(B, S, D)).astype(h.dtype) * g
    return out @ w_o


def moe_swiglu(h, router, w_gate, w_up, w_down):
    B, S, D = h.shape
    T = B * S
    x2 = h.reshape(T, D)
    logits = jnp.einsum('td,de->te', x2.astype(jnp.float32), router.astype(jnp.float32))
    top_logit, top_idx = lax.top_k(logits, TOP_K)
    gates = jax.nn.softmax(top_logit, axis=-1).astype(h.dtype)
    flat_e = top_idx.reshape(-1)
    onehot = jax.nn.one_hot(flat_e, N_EXPERTS, dtype=jnp.int32)
    rank = jnp.take_along_axis(jnp.cumsum(onehot, axis=0), flat_e[:, None], axis=1)[:, 0] - 1
    counts = jnp.sum(onehot, axis=0)
    padded = (counts + MOE_BLOCK - 1) // MOE_BLOCK * MOE_BLOCK
    pad_end = jnp.cumsum(padded)
    pad_start = pad_end - padded
    dest = pad_start[flat_e] + rank
    n_blocks = -(-(T * TOP_K) // MOE_BLOCK) + N_EXPERTS
    tok = jnp.arange(T * TOP_K) // TOP_K
    buf = jnp.zeros((n_blocks * MOE_BLOCK, D), h.dtype).at[dest].set(x2[tok])
    block_start = jnp.arange(n_blocks) * MOE_BLOCK
    block_expert = jnp.minimum(jnp.sum(block_start[:, None] >= pad_end[None, :], axis=1), N_EXPERTS - 1)

    def expert_block(args):
        xb, e = args
        return swiglu(xb, w_gate[e], w_up[e], w_down[e])

    out = lax.map(expert_block, (buf.reshape(n_blocks, MOE_BLOCK, D), block_expert)).reshape(-1, D)
    y = jnp.einsum('tkd,tk->td', out[dest].reshape(T, TOP_K, D), gates)
    return y.reshape(B, S, D)


def setup_inputs(seed: int = 0) -> dict:
    key = jax.random.key(seed)
    ks = iter(jax.random.split(key, 48))
    D = D_MODEL

    def nrm(shape, scale):
        return jax.random.normal(next(ks), shape, jnp.float32) * scale

    def gain(shape):
        return 1.0 + nrm(shape, 0.02)

    return {
        "x": nrm((BATCH, SEQ, D), 1.0),
        "norm_mix": gain((DEPTH, D)),
        "norm_ffn": gain((DEPTH, D)),
        "norm_final": gain((D,)),
        "a_w_in": nrm((N_A_LAYERS, D, 2 * A_WIDTH), D ** -0.5),
        "a_ln_g": gain((N_A_LAYERS, A_WIDTH)),
        "a_ln_b": nrm((N_A_LAYERS, A_WIDTH), 0.02),
        "a_w_s": nrm((N_A_LAYERS, A_GROUPS, CHUNK, CHUNK), 0.5 * CHUNK ** -0.5),
        "a_b_s": 1.0 + nrm((N_A_LAYERS, A_GROUPS, CHUNK), 0.1),
        "a_w_out": nrm((N_A_LAYERS, A_WIDTH, D), A_WIDTH ** -0.5),
        "b_w_in": nrm((N_B_LAYERS, D, B_WIDTH), D ** -0.5),
        "b_w_grp": nrm((N_B_LAYERS, B_GROUPS, B_GROUP_DIM, B_GROUP_DIM), B_GROUP_DIM ** -0.5),
        "b_scale": 1.0 + nrm((N_B_LAYERS, B_WIDTH), 0.1),
        "b_w_out": nrm((N_B_LAYERS, B_WIDTH, D), B_WIDTH ** -0.5),
        "c_mu": jax.random.uniform(next(ks), (N_C_LAYERS, N_SHIFT, D), jnp.float32),
        "c_w_rkv": nrm((N_C_LAYERS, 3, D, D), D ** -0.5),
        "c_w0": jax.random.uniform(next(ks), (N_C_LAYERS, D), jnp.float32, minval=-5.0, maxval=-0.5),
        "c_w1": nrm((N_C_LAYERS, D, DECAY_LORA), D ** -0.5),
        "c_w2": nrm((N_C_LAYERS, DECAY_LORA, D), 0.1 * DECAY_LORA ** -0.5),
        "c_a0": nrm((N_C_LAYERS, D), 0.1),
        "c_a1": nrm((N_C_LAYERS, D, AAA_LORA), D ** -0.5),
        "c_a2": nrm((N_C_LAYERS, AAA_LORA, D), 0.3 * AAA_LORA ** -0.5),
        "c_g1": nrm((N_C_LAYERS, D, GATE_LORA), D ** -0.5),
        "c_g2": nrm((N_C_LAYERS, GATE_LORA, D), GATE_LORA ** -0.5),
        "c_k_k": 0.85 + nrm((N_C_LAYERS, D), 0.1),
        "c_k_a": 1.0 + nrm((N_C_LAYERS, D), 0.1),
        "c_r_k": nrm((N_C_LAYERS, C_HEADS, C_HEAD_DIM), 0.1),
        "c_ln_g": gain((N_C_LAYERS, D)),
        "c_ln_b": nrm((N_C_LAYERS, D), 0.02),
        "c_w_o": nrm((N_C_LAYERS, D, D), D ** -0.5),
        "f_w_gate": nrm((N_DENSE_LAYERS, D, D_FF), D ** -0.5),
        "f_w_up": nrm((N_DENSE_LAYERS, D, D_FF), D ** -0.5),
        "f_w_down": nrm((N_DENSE_LAYERS, D_FF, D), D_FF ** -0.5),
        "m_router": nrm((N_MOE_LAYERS, D, N_EXPERTS), D ** -0.5),
        "m_w_gate": nrm((N_MOE_LAYERS, N_EXPERTS, D, D_FF), D ** -0.5),
        "m_w_up": nrm((N_MOE_LAYERS, N_EXPERTS, D, D_FF), D ** -0.5),
        "m_w_down": nrm((N_MOE_LAYERS, N_EXPERTS, D_FF, D), D_FF ** -0.5),
    }


def reference(x, norm_mix, norm_ffn, norm_final,
              a_w_in, a_ln_g, a_ln_b, a_w_s, a_b_s, a_w_out,
              b_w_in, b_w_grp, b_scale, b_w_out,
              c_mu, c_w_rkv, c_w0, c_w1, c_w2, c_a0, c_a1, c_a2, c_g1, c_g2,
              c_k_k, c_k_a, c_r_k, c_ln_g, c_ln_b, c_w_o,
              f_w_gate, f_w_up, f_w_down,
              m_router, m_w_gate, m_w_up, m_w_down):
    for layer in range(DEPTH):
        h = rmsnorm(x, norm_mix[layer])
        kind, j = layer % N_MIXERS, layer // N_MIXERS
        if kind == 0:
            y = chunked_gmlp(h, a_w_in[j], a_ln_g[j], a_ln_b[j], a_w_s[j], a_b_s[j], a_w_out[j])
        elif kind == 1:
            y = pooling_mixer(h, b_w_in[j], b_w_grp[j], b_scale[j], b_w_out[j])
        else:
            y = rwkv7_time_mix(h, c_mu[j], c_w_rkv[j], c_w0[j], c_w1[j], c_w2[j],
                               c_a0[j], c_a1[j], c_a2[j], c_g1[j], c_g2[j],
                               c_k_k[j], c_k_a[j], c_r_k[j], c_ln_g[j], c_ln_b[j], c_w_o[j])
        x = x + y
        h = rmsnorm(x, norm_ffn[layer])
        j = layer // 2
        if layer % 2 == 0:
            y = swiglu(h, f_w_gate[j], f_w_up[j], f_w_down[j])
        else:
            y = moe_swiglu(h, m_router[j], m_w_gate[j], m_w_up[j], m_w_down[j])
        x = x + y
    return rmsnorm(x, norm_final)
```

```python
import functools

import jax
import jax.numpy as jnp
from jax import lax
from jax.experimental import pallas as pl
from jax.experimental.pallas import tpu as pltpu

F32 = jnp.float32
BF16 = jnp.bfloat16

RMS_EPS = 1e-6
LN_EPS = 1e-5
GN_EPS = 64e-5

CHUNK = 128
A_GROUPS = 8
POOL_WINDOWS = (2, 4, 8, 16)
POOL_HALO = 16
HEAD_DIM = 64
N_EXPERTS = 8
TOP_K = 2

LANES = 128
SUBLANES = 8
VMEM_LIMIT = 56 * 1024 * 1024

TM_GMLP = 512
TM_FFN = 1024
TF_FFN = 512
TM_POOL = 512
TM_ROUTER = 512
TB_MOE = 512
TM_MOE = 512
TM_RWKV = 256
TT_SCAN = 32


def _cparams(*sem):
    return pltpu.CompilerParams(dimension_semantics=sem, vmem_limit_bytes=VMEM_LIMIT)


def _rms(x, g):
    return x * lax.rsqrt(jnp.mean(x * x, axis=-1, keepdims=True) + RMS_EPS) * g


def _bdot(a, b):
    return jnp.dot(a.astype(BF16), b.astype(BF16), preferred_element_type=F32)


def _split_dot(a, b):
    hi = a.astype(BF16)
    lo = (a - hi.astype(F32)).astype(BF16)
    return (jnp.dot(hi, b, preferred_element_type=F32)
            + jnp.dot(lo, b, preferred_element_type=F32))


def _gmlp_kernel(x_ref, gn_ref, win_ref, lng_ref, lnb_ref, ws_ref, bs_ref, wout_ref,
                 o_ref, us_scr, *, width, n_chunks):
    x = x_ref[...]
    h = _rms(x, gn_ref[...])
    z = _bdot(h, win_ref[...])
    z = 0.5 * z * (1.0 + lax.erf(z * (2.0 ** -0.5)))
    u = z[:, :width]
    v = z[:, width:]
    mu = jnp.mean(v, axis=-1, keepdims=True)
    vc = v - mu
    var = jnp.mean(vc * vc, axis=-1, keepdims=True)
    vb = (vc * lax.rsqrt(var + LN_EPS) * lng_ref[...] + lnb_ref[...]).astype(BF16)
    gd = width // A_GROUPS
    row = lax.broadcasted_iota(jnp.int32, (CHUNK, CHUNK), 0)
    col = lax.broadcasted_iota(jnp.int32, (CHUNK, CHUNK), 1)
    causal = row >= col
    for g in range(A_GROUPS):
        w = jnp.where(causal, ws_ref[g], 0.0).astype(BF16)
        rhs = jnp.concatenate(
            [vb[c * CHUNK:(c + 1) * CHUNK, g * gd:(g + 1) * gd] for c in range(n_chunks)], axis=1)
        s = jnp.dot(w, rhs, preferred_element_type=F32)
        bias = bs_ref[g]
        for c in range(n_chunks):
            sc = s[:, c * gd:(c + 1) * gd] + bias
            uc = u[c * CHUNK:(c + 1) * CHUNK, g * gd:(g + 1) * gd]
            us_scr[c * CHUNK:(c + 1) * CHUNK, g * gd:(g + 1) * gd] = (uc * sc).astype(BF16)
    o_ref[...] = x + jnp.dot(us_scr[...], wout_ref[...].astype(BF16), preferred_element_type=F32)


def _gmlp_mixer(x, gn, w_in, ln_g, ln_b, w_s, b_s, w_out):
    t, d = x.shape
    width = w_in.shape[1] // 2
    gd = width // A_GROUPS
    tm = min(TM_GMLP, t)
    n_chunks = tm // CHUNK
    bias = jnp.broadcast_to(b_s[:, :, None], (A_GROUPS, CHUNK, gd))
    const = lambda *shape: pl.BlockSpec(shape, lambda i: (0,) * len(shape))
    return pl.pallas_call(
        functools.partial(_gmlp_kernel, width=width, n_chunks=n_chunks),
        out_shape=jax.ShapeDtypeStruct((t, d), F32),
        grid=(t // tm,),
        in_specs=[
            pl.BlockSpec((tm, d), lambda i: (i, 0)),
            const(1, d), const(d, 2 * width), const(1, width), const(1, width),
            const(A_GROUPS, CHUNK, CHUNK), const(A_GROUPS, CHUNK, gd), const(width, d),
        ],
        out_specs=pl.BlockSpec((tm, d), lambda i: (i, 0)),
        scratch_shapes=[pltpu.VMEM((tm, width), BF16)],
        compiler_params=_cparams("parallel"),
        name="gmlp_mixer",
    )(x, gn.reshape(1, d), w_in, ln_g.reshape(1, width), ln_b.reshape(1, width), w_s, bias, w_out)


def _pool_kernel(x_ref, xh_ref, gn_ref, win_ref, wgrp_ref, scale_ref, wout_ref, o_ref, d_scr,
                 *, seq, tm):
    i = pl.program_id(0)
    x = x_ref[...]
    width = win_ref.shape[1]
    gd = width // len(POOL_WINDOWS)
    t0 = (i * tm) % seq
    xa = jnp.concatenate([xh_ref[...], x], axis=0)
    p_all = _bdot(_rms(xa, gn_ref[...]), win_ref[...])
    r = lax.broadcasted_iota(jnp.int32, (POOL_HALO + tm, 1), 0)
    p_all = jnp.where(r + (t0 - POOL_HALO) >= 0, p_all, 0.0)
    pos = lax.broadcasted_iota(jnp.int32, (tm, 1), 0) + (t0 + 1)

    sums = p_all
    have = 1
    for gi, win in enumerate(POOL_WINDOWS):
        while have < win:
            sums = sums[have:] + sums[:-have]
            have *= 2
        lo, hi = gi * gd, (gi + 1) * gd
        s = sums[POOL_HALO - (win - 1):POOL_HALO - (win - 1) + tm, lo:hi]
        count = jnp.minimum(pos, win).astype(F32)
        d_scr[:, lo:hi] = (s / count - p_all[POOL_HALO:, lo:hi]).astype(BF16)
    ys = []
    for gi in range(len(POOL_WINDOWS)):
        lo, hi = gi * gd, (gi + 1) * gd
        ys.append(jnp.dot(d_scr[:, lo:hi], wgrp_ref[gi].astype(BF16), preferred_element_type=F32))
    y = jnp.concatenate(ys, axis=1) * scale_ref[...]
    o_ref[...] = x + _bdot(y, wout_ref[...])


def _pool_mixer(x, gn, w_in, w_grp, scale, w_out, seq):
    t, d = x.shape
    width = w_in.shape[1]
    ng = len(POOL_WINDOWS)
    gd = width // ng
    tm = min(TM_POOL, seq)
    hb = tm // POOL_HALO
    const = lambda *shape: pl.BlockSpec(shape, lambda i: (0,) * len(shape))
    return pl.pallas_call(
        functools.partial(_pool_kernel, seq=seq, tm=tm),
        out_shape=jax.ShapeDtypeStruct((t, d), F32),
        grid=(t // tm,),
        in_specs=[
            pl.BlockSpec((tm, d), lambda i: (i, 0)),
            pl.BlockSpec((POOL_HALO, d), lambda i: (jnp.maximum(i * hb - 1, 0), 0)),
            const(1, d), const(d, width), const(ng, gd, gd), const(1, width), const(width, d),
        ],
        out_specs=pl.BlockSpec((tm, d), lambda i: (i, 0)),
        scratch_shapes=[pltpu.VMEM((tm, width), BF16)],
        compiler_params=_cparams("parallel"),
        name="pool_mixer",
    )(x, x, gn.reshape(1, d), w_in, w_grp, scale.reshape(1, width), w_out)


def _ffn_kernel(x_ref, gn_ref, wg_ref, wu_ref, wd_ref, o_ref, h_scr, acc_scr):
    j = pl.program_id(1)

    @pl.when(j == 0)
    def _():
        h_scr[...] = _rms(x_ref[...], gn_ref[...]).astype(BF16)
        acc_scr[...] = jnp.zeros_like(acc_scr)

    h = h_scr[...]
    gate = jnp.dot(h, wg_ref[...].astype(BF16), preferred_element_type=F32)
    up = jnp.dot(h, wu_ref[...].astype(BF16), preferred_element_type=F32)
    act = (gate * jax.nn.sigmoid(gate) * up).astype(BF16)
    acc_scr[...] += jnp.dot(act, wd_ref[...].astype(BF16), preferred_element_type=F32)

    @pl.when(j == pl.num_programs(1) - 1)
    def _():
        o_ref[...] = x_ref[...] + acc_scr[...]


def _ffn_dense(x, gn, w_gate, w_up, w_down):
    t, d = x.shape
    f = w_gate.shape[1]
    tm = min(TM_FFN, t)
    tf = min(TF_FFN, f)
    return pl.pallas_call(
        _ffn_kernel,
        out_shape=jax.ShapeDtypeStruct((t, d), F32),
        grid=(t // tm, f // tf),
        in_specs=[
            pl.BlockSpec((tm, d), lambda i, j: (i, 0)),
            pl.BlockSpec((1, d), lambda i, j: (0, 0)),
            pl.BlockSpec((d, tf), lambda i, j: (0, j)),
            pl.BlockSpec((d, tf), lambda i, j: (0, j)),
            pl.BlockSpec((tf, d), lambda i, j: (j, 0)),
        ],
        out_specs=pl.BlockSpec((tm, d), lambda i, j: (i, 0)),
        scratch_shapes=[pltpu.VMEM((tm, d), BF16), pltpu.VMEM((tm, d), F32)],
        compiler_params=_cparams("parallel", "arbitrary"),
        name="ffn_dense",
    )(x, gn.reshape(1, d), w_gate, w_up, w_down)


def _router_kernel(x_ref, gn_ref, wr_ref, idx_ref, gate_ref):
    h = _rms(x_ref[...], gn_ref[...])
    logits = jnp.dot(h, wr_ref[...], preferred_element_type=F32, precision=lax.Precision.HIGHEST)
    lane = lax.broadcasted_iota(jnp.int32, logits.shape, 1)
    lane_f = lane.astype(F32)
    neg = jnp.float32(-jnp.inf)
    logits = jnp.where(lane < N_EXPERTS, logits, neg)
    m1 = jnp.max(logits, axis=-1, keepdims=True)
    i1 = jnp.min(jnp.where(logits == m1, lane_f, float(LANES)), axis=-1, keepdims=True)
    rest = jnp.where(lane_f == i1, neg, logits)
    m2 = jnp.max(rest, axis=-1, keepdims=True)
    i2 = jnp.min(jnp.where(rest == m2, lane_f, float(LANES)), axis=-1, keepdims=True)
    e = jnp.exp(m2 - m1)
    den = 1.0 + e
    idx_ref[...] = jnp.where(lane == 0, i1, jnp.where(lane == 1, i2, 0.0)).astype(jnp.int32)
    gate_ref[...] = jnp.where(lane == 0, 1.0 / den, jnp.where(lane == 1, e / den, 0.0))


def _moe_router(x, gn, router):
    t, d = x.shape
    tm = min(TM_ROUTER, t)
    wr = jnp.zeros((d, LANES), F32).at[:, :N_EXPERTS].set(router)
    return pl.pallas_call(
        _router_kernel,
        out_shape=(jax.ShapeDtypeStruct((t, LANES), jnp.int32), jax.ShapeDtypeStruct((t, LANES), F32)),
        grid=(t // tm,),
        in_specs=[
            pl.BlockSpec((tm, d), lambda i: (i, 0)),
            pl.BlockSpec((1, d), lambda i: (0, 0)),
            pl.BlockSpec((d, LANES), lambda i: (0, 0)),
        ],
        out_specs=(pl.BlockSpec((tm, LANES), lambda i: (i, 0)), pl.BlockSpec((tm, LANES), lambda i: (i, 0))),
        compiler_params=_cparams("parallel"),
        name="moe_router",
    )(x, gn.reshape(1, d), wr)


def _row_copy(src, src_row, dst, dst_row, sem):
    return pltpu.make_async_copy(src.at[pl.ds(src_row, 1)], dst.at[pl.ds(dst_row, 1)], sem)


def _dispatch_kernel(dest_ref, x_ref, gn_ref, buf_in_ref, buf_ref, h_scr, sem):
    del buf_in_ref
    h_scr[...] = _rms(x_ref[...], gn_ref[...])
    n = h_scr.shape[0]

    def start(r, c):
        for k in range(TOP_K):
            _row_copy(h_scr, r, buf_ref, dest_ref[TOP_K * r + k], sem).start()
        return c

    lax.fori_loop(0, n, start, 0)

    def wait(r, c):
        for k in range(TOP_K):
            _row_copy(h_scr, r, buf_ref, dest_ref[TOP_K * r + k], sem).wait()
        return c

    lax.fori_loop(0, n, wait, 0)


def _moe_dispatch(x, gn, dest, n_rows):
    t, d = x.shape
    tb = min(TB_MOE, t)
    buf0 = jnp.zeros((n_rows, d), F32)
    return pl.pallas_call(
        _dispatch_kernel,
        out_shape=jax.ShapeDtypeStruct((n_rows, d), F32),
        grid=(t // tb,),
        in_specs=[
            pl.BlockSpec((TOP_K * tb,), lambda i: (i,), memory_space=pltpu.SMEM),
            pl.BlockSpec((tb, d), lambda i: (i, 0)),
            pl.BlockSpec((1, d), lambda i: (0, 0)),
            pl.BlockSpec(memory_space=pl.ANY),
        ],
        out_specs=pl.BlockSpec(memory_space=pl.ANY),
        scratch_shapes=[pltpu.VMEM((tb, d), F32), pltpu.SemaphoreType.DMA],
        input_output_aliases={3: 0},
        compiler_params=_cparams("arbitrary"),
        name="moe_dispatch",
    )(dest, x, gn.reshape(1, d), buf0)


def _moe_ffn_kernel(be_ref, nu_ref, xg_ref, wg_ref, wu_ref, wd_ref, o_ref, acc_scr):
    i = pl.program_id(0)
    j = pl.program_id(1)
    used = i < nu_ref[0]

    @pl.when(j == 0)
    def _():
        acc_scr[...] = jnp.zeros_like(acc_scr)

    @pl.when(used)
    def _():
        h = xg_ref[...].astype(BF16)
        gate = jnp.dot(h, wg_ref[...].astype(BF16), preferred_element_type=F32)
        up = jnp.dot(h, wu_ref[...].astype(BF16), preferred_element_type=F32)
        act = (gate * jax.nn.sigmoid(gate) * up).astype(BF16)
        acc_scr[...] += jnp.dot(act, wd_ref[...].astype(BF16), preferred_element_type=F32)

    @pl.when(j == pl.num_programs(1) - 1)
    def _():
        o_ref[...] = acc_scr[...]


def _moe_ffn(buf, block_expert, n_used, w_gate, w_up, w_down):
    n_rows, d = buf.shape
    f = w_gate.shape[2]
    tm = TM_MOE
    tf = min(TF_FFN, f)
    nb, nf = n_rows // tm, f // tf

    def fj(i, j, nu):
        return jnp.where(i < nu[0], j, nf - 1)

    def row(i, nu):
        return jnp.minimum(i, nu[0] - 1)

    grid_spec = pltpu.PrefetchScalarGridSpec(
        num_scalar_prefetch=2,
        grid=(nb, nf),
        in_specs=[
            pl.BlockSpec((tm, d), lambda i, j, be, nu: (row(i, nu), 0)),
            pl.BlockSpec((None, d, tf), lambda i, j, be, nu: (be[i], 0, fj(i, j, nu))),
            pl.BlockSpec((None, d, tf), lambda i, j, be, nu: (be[i], 0, fj(i, j, nu))),
            pl.BlockSpec((None, tf, d), lambda i, j, be, nu: (be[i], fj(i, j, nu), 0)),
        ],
        out_specs=pl.BlockSpec((tm, d), lambda i, j, be, nu: (i, 0)),
        scratch_shapes=[pltpu.VMEM((tm, d), F32)],
    )
    return pl.pallas_call(
        _moe_ffn_kernel,
        out_shape=jax.ShapeDtypeStruct((n_rows, d), F32),
        grid_spec=grid_spec,
        compiler_params=_cparams("arbitrary", "arbitrary"),
        name="moe_ffn",
    )(block_expert, n_used, buf, w_gate, w_up, w_down)


def _combine_kernel(dest_ref, x_ref, gate_ref, gn_ref, y_ref, o_ref, rows_scr, sem, *, final_norm):
    n = x_ref.shape[0]

    def start(r, c):
        for k in range(TOP_K):
            _row_copy(y_ref, dest_ref[TOP_K * r + k], rows_scr.at[k], r, sem).start()
        return c

    lax.fori_loop(0, n, start, 0)

    def wait(r, c):
        for k in range(TOP_K):
            _row_copy(y_ref, dest_ref[TOP_K * r + k], rows_scr.at[k], r, sem).wait()
        return c

    lax.fori_loop(0, n, wait, 0)
    gates = gate_ref[...]
    out = x_ref[...]
    for k in range(TOP_K):
        out = out + rows_scr[k] * gates[:, k:k + 1]
    if final_norm:
        out = _rms(out, gn_ref[...])
    o_ref[...] = out


def _moe_combine(x, y_grouped, dest, gates, gn_final, final_norm):
    t, d = x.shape
    tb = min(TB_MOE, t)
    return pl.pallas_call(
        functools.partial(_combine_kernel, final_norm=final_norm),
        out_shape=jax.ShapeDtypeStruct((t, d), F32),
        grid=(t // tb,),
        in_specs=[
            pl.BlockSpec((TOP_K * tb,), lambda i: (i,), memory_space=pltpu.SMEM),
            pl.BlockSpec((tb, d), lambda i: (i, 0)),
            pl.BlockSpec((tb, LANES), lambda i: (i, 0)),
            pl.BlockSpec((1, d), lambda i: (0, 0)),
            pl.BlockSpec(memory_space=pl.ANY),
        ],
        out_specs=pl.BlockSpec((tb, d), lambda i: (i, 0)),
        scratch_shapes=[pltpu.VMEM((TOP_K, tb, d), F32), pltpu.SemaphoreType.DMA],
        compiler_params=_cparams("arbitrary"),
        name="moe_combine",
    )(dest, x, gates, gn_final.reshape(1, d), y_grouped)


def _moe_layer(x, gn, router, w_gate, w_up, w_down, gn_final, final_norm):
    t, d = x.shape
    tm = TM_MOE
    idx, gates = _moe_router(x, gn, router)
    flat_e = idx[:, :TOP_K].reshape(-1)
    onehot = (flat_e[:, None] == jnp.arange(N_EXPERTS, dtype=jnp.int32)[None, :]).astype(jnp.int32)
    csum = jnp.cumsum(onehot, axis=0)
    rank = jnp.sum(csum * onehot, axis=1) - 1
    counts = csum[-1]
    padded = (counts + tm - 1) // tm * tm
    pad_end = jnp.cumsum(padded)
    pad_start = pad_end - padded
    dest = (jnp.sum(pad_start[None, :] * onehot, axis=1) + rank).astype(jnp.int32)
    nb = -(-(t * TOP_K) // tm) + N_EXPERTS
    block_start = jnp.arange(nb, dtype=jnp.int32) * tm
    block_expert = jnp.minimum(
        jnp.sum((block_start[:, None] >= pad_end[None, :]).astype(jnp.int32), axis=1), N_EXPERTS - 1)
    n_used = (pad_end[-1] // tm).astype(jnp.int32).reshape(1)
    buf = _moe_dispatch(x, gn, dest, nb * tm)
    y_grouped = _moe_ffn(buf, block_expert.astype(jnp.int32), n_used, w_gate, w_up, w_down)
    return _moe_combine(x, y_grouped, dest, gates, gn_final, final_norm)


def _head_sum(a, hsum_ref, hexp_ref):
    return _split_dot(_split_dot(a, hsum_ref[...]), hexp_ref[...])


def _rwkv_proj_kernel(x_ref, xh_ref, gn_ref, mu_ref, wrkv_ref, w0_ref, w1_ref, w2_ref,
                      a0_ref, a1_ref, a2_ref, g1_ref, g2_ref, kk_ref, ka_ref, hsum_ref, hexp_ref,
                      r_ref, dec_ref, k_ref, v_ref, na_ref, b_ref, g_ref, *, seq, tm):
    i = pl.program_id(0)
    gn = gn_ref[...]
    h = _rms(x_ref[...], gn)
    h_last = _rms(xh_ref[SUBLANES - 1:SUBLANES, :], gn)
    h_last = jnp.where((i * tm) % seq == 0, 0.0, h_last)
    rowi = lax.broadcasted_iota(jnp.int32, (tm, 1), 0)
    h_prev = jnp.where(rowi == 0, h_last, pltpu.roll(h, 1, 0))
    xx = h_prev - h
    mu = mu_ref[...]
    xs = [h + xx * mu[n:n + 1, :] for n in range(6)]
    r = _bdot(xs[0], wrkv_ref[0])
    k = _bdot(xs[1], wrkv_ref[1])
    v = _bdot(xs[2], wrkv_ref[2])
    wl = w0_ref[...] + _bdot(jnp.tanh(_bdot(xs[3], w1_ref[...])), w2_ref[...])
    w = -jax.nn.softplus(-wl) - 0.5
    a = jax.nn.sigmoid(a0_ref[...] + _bdot(_bdot(xs[4], a1_ref[...]), a2_ref[...]))
    g = _bdot(jax.nn.sigmoid(_bdot(xs[5], g1_ref[...])), g2_ref[...])
    kk = k * kk_ref[...]
    norm = jnp.sqrt(_head_sum(kk * kk, hsum_ref, hexp_ref))
    kk = kk / jnp.maximum(norm, 1e-12)
    r_ref[...] = r
    dec_ref[...] = jnp.exp(-jnp.exp(w))
    k_ref[...] = k * (1.0 + (a - 1.0) * ka_ref[...])
    v_ref[...] = v
    na_ref[...] = -kk
    b_ref[...] = kk * a
    g_ref[...] = g


def _scan_kernel(r_ref, w_ref, k_ref, v_ref, a_ref, b_ref, y_ref, s_scr):
    n = s_scr.shape[0]

    @pl.when(pl.program_id(0) == 0)
    def _():
        s_scr[...] = jnp.zeros_like(s_scr)

    def step(t, c):
        v = v_ref[t]
        sa = jnp.zeros_like(v)
        for j in range(n):
            sa = sa + s_scr[j] * a_ref[t, j:j + 1, :]
        y = jnp.zeros_like(v)
        for j in range(n):
            s_new = (s_scr[j] * w_ref[t, j:j + 1, :] + sa * b_ref[t, j:j + 1, :]
                     + v * k_ref[t, j:j + 1, :])
            s_scr[j] = s_new
            y = y + s_new * r_ref[t, j:j + 1, :]
        y_ref[t] = y
        return c

    lax.fori_loop(0, r_ref.shape[0], step, 0)


def _rwkv_out_kernel(x_ref, y_ref, r_ref, k_ref, v_ref, g_ref, rk_ref, lng_ref, lnb_ref,
                     wo_ref, hsum_ref, hexp_ref, o_ref):
    y = y_ref[...]
    inv_n = 1.0 / HEAD_DIM
    mean = _head_sum(y, hsum_ref, hexp_ref) * inv_n
    yc = y - mean
    var = _head_sum(yc * yc, hsum_ref, hexp_ref) * inv_n
    yn = yc * lax.rsqrt(var + GN_EPS) * lng_ref[...] + lnb_ref[...]
    bonus = _head_sum(r_ref[...] * k_ref[...] * rk_ref[...], hsum_ref, hexp_ref) * v_ref[...]
    out = (yn + bonus) * g_ref[...]
    o_ref[...] = x_ref[...] + _bdot(out, wo_ref[...])


def _rwkv_mixer(x, gn, mu, w_rkv, w0, w1, w2, a0, a1, a2, g1, g2, k_k, k_a, r_k, ln_g, ln_b, w_o,
                batch, seq):
    t, d = x.shape
    heads = d // HEAD_DIM
    tm = min(TM_RWKV, seq)
    hb = tm // SUBLANES
    head_of = jnp.arange(d, dtype=jnp.int32) // HEAD_DIM
    hsum = (head_of[:, None] == jnp.arange(LANES, dtype=jnp.int32)[None, :]).astype(BF16)
    hexp = hsum.T
    const = lambda *shape: pl.BlockSpec(shape, lambda i: (0,) * len(shape))
    tok = pl.BlockSpec((tm, d), lambda i: (i, 0))
    vec = lambda a: a.reshape(1, d)
    r, dec, k, v, na, b, g = pl.pallas_call(
        functools.partial(_rwkv_proj_kernel, seq=seq, tm=tm),
        out_shape=tuple(jax.ShapeDtypeStruct((t, d), F32) for _ in range(7)),
        grid=(t // tm,),
        in_specs=[
            tok,
            pl.BlockSpec((SUBLANES, d), lambda i: (jnp.maximum(i * hb - 1, 0), 0)),
            const(1, d), const(6, d), const(3, d, d),
            const(1, d), const(d, w1.shape[1]), const(w2.shape[0], d),
            const(1, d), const(d, a1.shape[1]), const(a2.shape[0], d),
            const(d, g1.shape[1]), const(g2.shape[0], d),
            const(1, d), const(1, d), const(d, LANES), const(LANES, d),
        ],
        out_specs=tuple(tok for _ in range(7)),
        compiler_params=_cparams("parallel"),
        name="rwkv_proj",
    )(x, x, vec(gn), mu, w_rkv, vec(w0), w1, w2, vec(a0), a1, a2, g1, g2, vec(k_k), vec(k_a), hsum, hexp)

    def to_scan(z):
        return jnp.transpose(z.reshape(batch, seq, heads, HEAD_DIM), (1, 3, 0, 2)).reshape(
            seq, HEAD_DIM, batch * heads)

    bh = batch * heads
    tt = min(TT_SCAN, seq)
    blk = pl.BlockSpec((tt, HEAD_DIM, bh), lambda i: (i, 0, 0))
    y = pl.pallas_call(
        _scan_kernel,
        out_shape=jax.ShapeDtypeStruct((seq, HEAD_DIM, bh), F32),
        grid=(seq // tt,),
        in_specs=[blk] * 6,
        out_specs=blk,
        scratch_shapes=[pltpu.VMEM((HEAD_DIM, HEAD_DIM, bh), F32)],
        compiler_params=_cparams("arbitrary"),
        name="rwkv_scan",
    )(to_scan(r), to_scan(dec), to_scan(k), to_scan(v), to_scan(na), to_scan(b))
    y = jnp.transpose(y.reshape(seq, HEAD_DIM, batch, heads), (2, 0, 3, 1)).reshape(t, d)

    return pl.pallas_call(
        _rwkv_out_kernel,
        out_shape=jax.ShapeDtypeStruct((t, d), F32),
        grid=(t // tm,),
        in_specs=[tok] * 6 + [const(1, d), const(1, d), const(1, d), const(d, d),
                              const(d, LANES), const(LANES, d)],
        out_specs=tok,
        compiler_params=_cparams("parallel"),
        name="rwkv_out",
    )(x, y, r, k, v, g, r_k.reshape(1, d), vec(ln_g), vec(ln_b), w_o, hsum, hexp)


def kernel(x, norm_mix, norm_ffn, norm_final, a_w_in, a_ln_g, a_ln_b, a_w_s, a_b_s, a_w_out, b_w_in, b_w_grp, b_scale, b_w_out, c_mu, c_w_rkv, c_w0, c_w1, c_w2, c_a0, c_a1, c_a2, c_g1, c_g2, c_k_k, c_k_a, c_r_k, c_ln_g, c_ln_b, c_w_o, f_w_gate, f_w_up, f_w_down, m_router, m_w_gate, m_w_up, m_w_down):
    batch, seq, d = x.shape
    depth = norm_mix.shape[0]
    bf = lambda w: w.astype(BF16)
    xt = x.reshape(batch * seq, d)
    for layer in range(depth):
        kind, j = layer % 3, layer // 3
        if kind == 0:
            xt = _gmlp_mixer(xt, norm_mix[layer], bf(a_w_in[j]), a_ln_g[j], a_ln_b[j], a_w_s[j], a_b_s[j],
                             bf(a_w_out[j]))
        elif kind == 1:
            xt = _pool_mixer(xt, norm_mix[layer], bf(b_w_in[j]), bf(b_w_grp[j]), b_scale[j], bf(b_w_out[j]), seq)
        else:
            xt = _rwkv_mixer(xt, norm_mix[layer], c_mu[j], bf(c_w_rkv[j]), c_w0[j], bf(c_w1[j]), bf(c_w2[j]),
                             c_a0[j], bf(c_a1[j]), bf(c_a2[j]), bf(c_g1[j]), bf(c_g2[j]),
                             c_k_k[j], c_k_a[j], c_r_k[j], c_ln_g[j], c_ln_b[j], bf(c_w_o[j]), batch, seq)
        j = layer // 2
        last = layer == depth - 1
        if layer % 2 == 0:
            xt = _ffn_dense(xt, norm_ffn[layer], bf(f_w_gate[j]), bf(f_w_up[j]), bf(f_w_down[j]))
            if last:
                xt = _final_norm(xt, norm_final)
        else:
            xt = _moe_layer(xt, norm_ffn[layer], m_router[j], bf(m_w_gate[j]), bf(m_w_up[j]), bf(m_w_down[j]),
                            norm_final, last)
    return xt.reshape(batch, seq, d)


def _final_norm_kernel(x_ref, gn_ref, o_ref):
    o_ref[...] = _rms(x_ref[...], gn_ref[...])


def _final_norm(x, gn):
    t, d = x.shape
    tm = min(TM_FFN, t)
    return pl.pallas_call(
        _final_norm_kernel,
        out_shape=jax.ShapeDtypeStruct((t, d), F32),
        grid=(t // tm,),
        in_specs=[pl.BlockSpec((tm, d), lambda i: (i, 0)), pl.BlockSpec((1, d), lambda i: (0, 0))],
        out_specs=pl.BlockSpec((tm, d), lambda i: (i, 0)),
        compiler_params=_cparams("parallel"),
        name="final_norm",
    )(x, gn.reshape(1, d))
```

```python
import functools

import jax
import jax.numpy as jnp
from jax import lax
from jax.experimental import pallas as pl
from jax.experimental.pallas import tpu as pltpu

F32 = jnp.float32
BF16 = jnp.bfloat16

RMS_EPS = 1e-6
LN_EPS = 1e-5
GN_EPS = 64e-5

CHUNK = 128
A_GROUPS = 8
POOL_WINDOWS = (2, 4, 8, 16)
POOL_HALO = 16
HEAD_DIM = 64
N_EXPERTS = 8
TOP_K = 2

LANES = 128
SUBLANES = 8
VMEM_LIMIT = 56 * 1024 * 1024

TM_GMLP = 512
TM_FFN = 1024
TF_FFN = 512
TM_POOL = 512
TM_ROUTER = 512
TB_MOE = 512
TM_MOE = 512
DMA_UNROLL = 8
TM_RWKV = 256
TT_SCAN = 32


def _cparams(*sem):
    return pltpu.CompilerParams(dimension_semantics=sem, vmem_limit_bytes=VMEM_LIMIT)


def _rms(x, g):
    return x * lax.rsqrt(jnp.mean(x * x, axis=-1, keepdims=True) + RMS_EPS) * g


def _bdot(a, b):
    return jnp.dot(a.astype(BF16), b.astype(BF16), preferred_element_type=F32)


def _split_dot(a, b):
    hi = a.astype(BF16)
    lo = (a - hi.astype(F32)).astype(BF16)
    return (jnp.dot(hi, b, preferred_element_type=F32)
            + jnp.dot(lo, b, preferred_element_type=F32))


def _gmlp_kernel(x_ref, gn_ref, win_ref, lng_ref, lnb_ref, ws_ref, bs_ref, wout_ref,
                 o_ref, us_scr, *, width, n_chunks):
    x = x_ref[...]
    h = _rms(x, gn_ref[...])
    z = _bdot(h, win_ref[...])
    z = 0.5 * z * (1.0 + lax.erf(z * (2.0 ** -0.5)))
    u = z[:, :width]
    v = z[:, width:]
    mu = jnp.mean(v, axis=-1, keepdims=True)
    vc = v - mu
    var = jnp.mean(vc * vc, axis=-1, keepdims=True)
    vb = (vc * lax.rsqrt(var + LN_EPS) * lng_ref[...] + lnb_ref[...]).astype(BF16)
    gd = width // A_GROUPS
    row = lax.broadcasted_iota(jnp.int32, (CHUNK, CHUNK), 0)
    col = lax.broadcasted_iota(jnp.int32, (CHUNK, CHUNK), 1)
    causal = row >= col
    for g in range(A_GROUPS):
        w = jnp.where(causal, ws_ref[g], 0.0).astype(BF16)
        rhs = jnp.concatenate(
            [vb[c * CHUNK:(c + 1) * CHUNK, g * gd:(g + 1) * gd] for c in range(n_chunks)], axis=1)
        s = jnp.dot(w, rhs, preferred_element_type=F32)
        bias = bs_ref[g]
        for c in range(n_chunks):
            sc = s[:, c * gd:(c + 1) * gd] + bias
            uc = u[c * CHUNK:(c + 1) * CHUNK, g * gd:(g + 1) * gd]
            us_scr[c * CHUNK:(c + 1) * CHUNK, g * gd:(g + 1) * gd] = (uc * sc).astype(BF16)
    o_ref[...] = x + jnp.dot(us_scr[...], wout_ref[...].astype(BF16), preferred_element_type=F32)


def _gmlp_mixer(x, gn, w_in, ln_g, ln_b, w_s, b_s, w_out):
    t, d = x.shape
    width = w_in.shape[1] // 2
    gd = width // A_GROUPS
    tm = min(TM_GMLP, t)
    n_chunks = tm // CHUNK
    bias = jnp.broadcast_to(b_s[:, :, None], (A_GROUPS, CHUNK, gd))
    const = lambda *shape: pl.BlockSpec(shape, lambda i: (0,) * len(shape))
    return pl.pallas_call(
        functools.partial(_gmlp_kernel, width=width, n_chunks=n_chunks),
        out_shape=jax.ShapeDtypeStruct((t, d), F32),
        grid=(t // tm,),
        in_specs=[
            pl.BlockSpec((tm, d), lambda i: (i, 0)),
            const(1, d), const(d, 2 * width), const(1, width), const(1, width),
            const(A_GROUPS, CHUNK, CHUNK), const(A_GROUPS, CHUNK, gd), const(width, d),
        ],
        out_specs=pl.BlockSpec((tm, d), lambda i: (i, 0)),
        scratch_shapes=[pltpu.VMEM((tm, width), BF16)],
        compiler_params=_cparams("parallel"),
        name="gmlp_mixer",
    )(x, gn.reshape(1, d), w_in, ln_g.reshape(1, width), ln_b.reshape(1, width), w_s, bias, w_out)


def _pool_kernel(x_ref, xh_ref, gn_ref, win_ref, wgrp_ref, scale_ref, wout_ref, o_ref, d_scr,
                 *, seq, tm):
    i = pl.program_id(0)
    x = x_ref[...]
    width = win_ref.shape[1]
    gd = width // len(POOL_WINDOWS)
    t0 = (i * tm) % seq
    xa = jnp.concatenate([xh_ref[...], x], axis=0)
    p_all = _bdot(_rms(xa, gn_ref[...]), win_ref[...])
    r = lax.broadcasted_iota(jnp.int32, (POOL_HALO + tm, 1), 0)
    p_all = jnp.where(r + (t0 - POOL_HALO) >= 0, p_all, 0.0)
    pos = lax.broadcasted_iota(jnp.int32, (tm, 1), 0) + (t0 + 1)

    sums = p_all
    have = 1
    for gi, win in enumerate(POOL_WINDOWS):
        while have < win:
            sums = sums[have:] + sums[:-have]
            have *= 2
        lo, hi = gi * gd, (gi + 1) * gd
        s = sums[POOL_HALO - (win - 1):POOL_HALO - (win - 1) + tm, lo:hi]
        count = jnp.minimum(pos, win).astype(F32)
        d_scr[:, lo:hi] = (s / count - p_all[POOL_HALO:, lo:hi]).astype(BF16)
    ys = []
    for gi in range(len(POOL_WINDOWS)):
        lo, hi = gi * gd, (gi + 1) * gd
        ys.append(jnp.dot(d_scr[:, lo:hi], wgrp_ref[gi].astype(BF16), preferred_element_type=F32))
    y = jnp.concatenate(ys, axis=1) * scale_ref[...]
    o_ref[...] = x + _bdot(y, wout_ref[...])


def _pool_mixer(x, gn, w_in, w_grp, scale, w_out, seq):
    t, d = x.shape
    width = w_in.shape[1]
    ng = len(POOL_WINDOWS)
    gd = width // ng
    tm = min(TM_POOL, seq)
    hb = tm // POOL_HALO
    const = lambda *shape: pl.BlockSpec(shape, lambda i: (0,) * len(shape))
    return pl.pallas_call(
        functools.partial(_pool_kernel, seq=seq, tm=tm),
        out_shape=jax.ShapeDtypeStruct((t, d), F32),
        grid=(t // tm,),
        in_specs=[
            pl.BlockSpec((tm, d), lambda i: (i, 0)),
            pl.BlockSpec((POOL_HALO, d), lambda i: (jnp.maximum(i * hb - 1, 0), 0)),
            const(1, d), const(d, width), const(ng, gd, gd), const(1, width), const(width, d),
        ],
        out_specs=pl.BlockSpec((tm, d), lambda i: (i, 0)),
        scratch_shapes=[pltpu.VMEM((tm, width), BF16)],
        compiler_params=_cparams("parallel"),
        name="pool_mixer",
    )(x, x, gn.reshape(1, d), w_in, w_grp, scale.reshape(1, width), w_out)


def _ffn_kernel(x_ref, gn_ref, wg_ref, wu_ref, wd_ref, o_ref, h_scr, acc_scr):
    j = pl.program_id(1)

    @pl.when(j == 0)
    def _():
        h_scr[...] = _rms(x_ref[...], gn_ref[...]).astype(BF16)
        acc_scr[...] = jnp.zeros_like(acc_scr)

    h = h_scr[...]
    gate = jnp.dot(h, wg_ref[...].astype(BF16), preferred_element_type=F32)
    up = jnp.dot(h, wu_ref[...].astype(BF16), preferred_element_type=F32)
    act = (gate * jax.nn.sigmoid(gate) * up).astype(BF16)
    acc_scr[...] += jnp.dot(act, wd_ref[...].astype(BF16), preferred_element_type=F32)

    @pl.when(j == pl.num_programs(1) - 1)
    def _():
        o_ref[...] = x_ref[...] + acc_scr[...]


def _ffn_dense(x, gn, w_gate, w_up, w_down, layer):
    t, d = x.shape
    f = w_gate.shape[2]
    tm = min(TM_FFN, t)
    tf = min(TF_FFN, f)
    return pl.pallas_call(
        _ffn_kernel,
        out_shape=jax.ShapeDtypeStruct((t, d), F32),
        grid=(t // tm, f // tf),
        in_specs=[
            pl.BlockSpec((tm, d), lambda i, j: (i, 0)),
            pl.BlockSpec((1, d), lambda i, j: (0, 0)),
            pl.BlockSpec((None, d, tf), lambda i, j: (layer, 0, j)),
            pl.BlockSpec((None, d, tf), lambda i, j: (layer, 0, j)),
            pl.BlockSpec((None, tf, d), lambda i, j: (layer, j, 0)),
        ],
        out_specs=pl.BlockSpec((tm, d), lambda i, j: (i, 0)),
        scratch_shapes=[pltpu.VMEM((tm, d), BF16), pltpu.VMEM((tm, d), F32)],
        compiler_params=_cparams("parallel", "arbitrary"),
        name="ffn_dense",
    )(x, gn.reshape(1, d), w_gate, w_up, w_down)


def _router_kernel(x_ref, gn_ref, wr_ref, idx_ref, gate_ref):
    h = _rms(x_ref[...], gn_ref[...])
    logits = jnp.dot(h, wr_ref[...], preferred_element_type=F32, precision=lax.Precision.HIGHEST)
    lane = lax.broadcasted_iota(jnp.int32, logits.shape, 1)
    lane_f = lane.astype(F32)
    neg = jnp.float32(-jnp.inf)
    logits = jnp.where(lane < N_EXPERTS, logits, neg)
    m1 = jnp.max(logits, axis=-1, keepdims=True)
    i1 = jnp.min(jnp.where(logits == m1, lane_f, float(LANES)), axis=-1, keepdims=True)
    rest = jnp.where(lane_f == i1, neg, logits)
    m2 = jnp.max(rest, axis=-1, keepdims=True)
    i2 = jnp.min(jnp.where(rest == m2, lane_f, float(LANES)), axis=-1, keepdims=True)
    e = jnp.exp(m2 - m1)
    den = 1.0 + e
    idx_ref[...] = jnp.where(lane == 0, i1, jnp.where(lane == 1, i2, 0.0)).astype(jnp.int32)
    gate_ref[...] = jnp.where(lane == 0, 1.0 / den, jnp.where(lane == 1, e / den, 0.0))


def _moe_router(x, gn, router):
    t, d = x.shape
    tm = min(TM_ROUTER, t)
    wr = jnp.zeros((d, LANES), F32).at[:, :N_EXPERTS].set(router)
    return pl.pallas_call(
        _router_kernel,
        out_shape=(jax.ShapeDtypeStruct((t, LANES), jnp.int32), jax.ShapeDtypeStruct((t, LANES), F32)),
        grid=(t // tm,),
        in_specs=[
            pl.BlockSpec((tm, d), lambda i: (i, 0)),
            pl.BlockSpec((1, d), lambda i: (0, 0)),
            pl.BlockSpec((d, LANES), lambda i: (0, 0)),
        ],
        out_specs=(pl.BlockSpec((tm, LANES), lambda i: (i, 0)), pl.BlockSpec((tm, LANES), lambda i: (i, 0))),
        compiler_params=_cparams("parallel"),
        name="moe_router",
    )(x, gn.reshape(1, d), wr)


def _tok_copy(src, src_tok, dst, dst_tok, sem):
    return pltpu.make_async_copy(
        src.at[pl.ds(pl.multiple_of(src_tok * SUBLANES, SUBLANES), SUBLANES)],
        dst.at[pl.ds(pl.multiple_of(dst_tok * SUBLANES, SUBLANES), SUBLANES)], sem)


def _to_tiles(ref, base, val):
    n, d = val.shape
    for c in range(d // LANES):
        ref[pl.ds(base + c, n, stride=SUBLANES), :] = val[:, c * LANES:(c + 1) * LANES]


def _from_tiles(ref, base, n, d):
    return jnp.concatenate(
        [ref[pl.ds(base + c, n, stride=SUBLANES), :] for c in range(d // LANES)], axis=1)


def _dispatch_kernel(dest_ref, x_ref, gn_ref, buf_in_ref, buf_ref, h_scr, sem):
    del buf_in_ref
    n = x_ref.shape[0]
    _to_tiles(h_scr, 0, _rms(x_ref[...], gn_ref[...]))

    def copies(r):
        return [_tok_copy(h_scr, r, buf_ref, dest_ref[TOP_K * r + k], sem) for k in range(TOP_K)]

    def start(g, c):
        for u in range(DMA_UNROLL):
            for cp in copies(g * DMA_UNROLL + u):
                cp.start()
        return c

    lax.fori_loop(0, n // DMA_UNROLL, start, 0)

    def wait(g, c):
        for u in range(DMA_UNROLL):
            for cp in copies(g * DMA_UNROLL + u):
                cp.wait()
        return c

    lax.fori_loop(0, n // DMA_UNROLL, wait, 0)


def _moe_dispatch(x, gn, dest, n_rows):
    t, d = x.shape
    tb = min(TB_MOE, t)
    tile_rows = d // LANES
    assert tile_rows == SUBLANES
    buf0 = jnp.zeros((n_rows * tile_rows, LANES), F32)
    return pl.pallas_call(
        _dispatch_kernel,
        out_shape=jax.ShapeDtypeStruct((n_rows * tile_rows, LANES), F32),
        grid=(t // tb,),
        in_specs=[
            pl.BlockSpec((TOP_K * tb,), lambda i: (i,), memory_space=pltpu.SMEM),
            pl.BlockSpec((tb, d), lambda i: (i, 0)),
            pl.BlockSpec((1, d), lambda i: (0, 0)),
            pl.BlockSpec(memory_space=pl.ANY),
        ],
        out_specs=pl.BlockSpec(memory_space=pl.ANY),
        scratch_shapes=[pltpu.VMEM((tb * tile_rows, LANES), F32), pltpu.SemaphoreType.DMA],
        input_output_aliases={3: 0},
        compiler_params=_cparams("arbitrary"),
        name="moe_dispatch",
    )(dest, x, gn.reshape(1, d), buf0)


def _moe_ffn_kernel(be_ref, nu_ref, xg_ref, wg_ref, wu_ref, wd_ref, o_ref, h_scr, acc_scr):
    i = pl.program_id(0)
    j = pl.program_id(1)
    used = i < nu_ref[0]
    tm, d = h_scr.shape

    @pl.when(j == 0)
    def _():
        h_scr[...] = _from_tiles(xg_ref, 0, tm, d).astype(BF16)
        acc_scr[...] = jnp.zeros_like(acc_scr)

    @pl.when(used)
    def _():
        h = h_scr[...]
        gate = jnp.dot(h, wg_ref[...].astype(BF16), preferred_element_type=F32)
        up = jnp.dot(h, wu_ref[...].astype(BF16), preferred_element_type=F32)
        act = (gate * jax.nn.sigmoid(gate) * up).astype(BF16)
        acc_scr[...] += jnp.dot(act, wd_ref[...].astype(BF16), preferred_element_type=F32)

    @pl.when(j == pl.num_programs(1) - 1)
    def _():
        _to_tiles(o_ref, 0, acc_scr[...])


def _moe_ffn(buf, block_expert, n_used, w_gate, w_up, w_down, layer, d):
    n_rows = buf.shape[0] // SUBLANES
    f = w_gate.shape[3]
    tm = TM_MOE
    tf = min(TF_FFN, f)
    nb, nf = n_rows // tm, f // tf

    def fj(i, j, nu):
        return jnp.where(i < nu[0], j, nf - 1)

    def row(i, nu):
        return jnp.minimum(i, nu[0] - 1)

    grid_spec = pltpu.PrefetchScalarGridSpec(
        num_scalar_prefetch=2,
        grid=(nb, nf),
        in_specs=[
            pl.BlockSpec((tm * SUBLANES, LANES), lambda i, j, be, nu: (row(i, nu), 0)),
            pl.BlockSpec((None, None, d, tf), lambda i, j, be, nu: (layer, be[i], 0, fj(i, j, nu))),
            pl.BlockSpec((None, None, d, tf), lambda i, j, be, nu: (layer, be[i], 0, fj(i, j, nu))),
            pl.BlockSpec((None, None, tf, d), lambda i, j, be, nu: (layer, be[i], fj(i, j, nu), 0)),
        ],
        out_specs=pl.BlockSpec((tm * SUBLANES, LANES), lambda i, j, be, nu: (i, 0)),
        scratch_shapes=[pltpu.VMEM((tm, d), BF16), pltpu.VMEM((tm, d), F32)],
    )
    return pl.pallas_call(
        _moe_ffn_kernel,
        out_shape=jax.ShapeDtypeStruct(buf.shape, F32),
        grid_spec=grid_spec,
        compiler_params=_cparams("arbitrary", "arbitrary"),
        name="moe_ffn",
    )(block_expert, n_used, buf, w_gate, w_up, w_down)


def _combine_kernel(dest_ref, x_ref, gate_ref, gn_ref, y_ref, o_ref, rows_scr, sem, *, final_norm):
    n, d = x_ref.shape

    def copies(r):
        return [_tok_copy(y_ref, dest_ref[TOP_K * r + k], rows_scr, k * n + r, sem) for k in range(TOP_K)]

    def start(g, c):
        for u in range(DMA_UNROLL):
            for cp in copies(g * DMA_UNROLL + u):
                cp.start()
        return c

    lax.fori_loop(0, n // DMA_UNROLL, start, 0)

    def wait(g, c):
        for u in range(DMA_UNROLL):
            for cp in copies(g * DMA_UNROLL + u):
                cp.wait()
        return c

    lax.fori_loop(0, n // DMA_UNROLL, wait, 0)
    gates = gate_ref[...]
    out = x_ref[...]
    for k in range(TOP_K):
        out = out + _from_tiles(rows_scr, k * n * SUBLANES, n, d) * gates[:, k:k + 1]
    if final_norm:
        out = _rms(out, gn_ref[...])
    o_ref[...] = out


def _moe_combine(x, y_grouped, dest, gates, gn_final, final_norm):
    t, d = x.shape
    tb = min(TB_MOE, t)
    return pl.pallas_call(
        functools.partial(_combine_kernel, final_norm=final_norm),
        out_shape=jax.ShapeDtypeStruct((t, d), F32),
        grid=(t // tb,),
        in_specs=[
            pl.BlockSpec((TOP_K * tb,), lambda i: (i,), memory_space=pltpu.SMEM),
            pl.BlockSpec((tb, d), lambda i: (i, 0)),
            pl.BlockSpec((tb, LANES), lambda i: (i, 0)),
            pl.BlockSpec((1, d), lambda i: (0, 0)),
            pl.BlockSpec(memory_space=pl.ANY),
        ],
        out_specs=pl.BlockSpec((tb, d), lambda i: (i, 0)),
        scratch_shapes=[pltpu.VMEM((TOP_K * tb * SUBLANES, LANES), F32), pltpu.SemaphoreType.DMA],
        compiler_params=_cparams("arbitrary"),
        name="moe_combine",
    )(dest, x, gates, gn_final.reshape(1, d), y_grouped)


def _moe_layer(x, gn, router, w_gate, w_up, w_down, layer, gn_final, final_norm):
    t, d = x.shape
    tm = TM_MOE
    idx, gates = _moe_router(x, gn, router)
    flat_e = idx[:, :TOP_K].reshape(-1)
    onehot = (flat_e[:, None] == jnp.arange(N_EXPERTS, dtype=jnp.int32)[None, :]).astype(jnp.int32)
    csum = jnp.cumsum(onehot, axis=0)
    rank = jnp.sum(csum * onehot, axis=1) - 1
    counts = csum[-1]
    padded = (counts + tm - 1) // tm * tm
    pad_end = jnp.cumsum(padded)
    pad_start = pad_end - padded
    dest = (jnp.sum(pad_start[None, :] * onehot, axis=1) + rank).astype(jnp.int32)
    nb = -(-(t * TOP_K) // tm) + N_EXPERTS
    block_start = jnp.arange(nb, dtype=jnp.int32) * tm
    block_expert = jnp.minimum(
        jnp.sum((block_start[:, None] >= pad_end[None, :]).astype(jnp.int32), axis=1), N_EXPERTS - 1)
    n_used = (pad_end[-1] // tm).astype(jnp.int32).reshape(1)
    buf = _moe_dispatch(x, gn, dest, nb * tm)
    y_grouped = _moe_ffn(buf, block_expert.astype(jnp.int32), n_used, w_gate, w_up, w_down, layer, d)
    return _moe_combine(x, y_grouped, dest, gates, gn_final, final_norm)


def _head_sum(a, hsum_ref, hexp_ref):
    return _split_dot(_split_dot(a, hsum_ref[...]), hexp_ref[...])


def _rwkv_proj_kernel(x_ref, xh_ref, gn_ref, mu_ref, wrkv_ref, w0_ref, w1_ref, w2_ref,
                      a0_ref, a1_ref, a2_ref, g1_ref, g2_ref, kk_ref, ka_ref, hsum_ref, hexp_ref,
                      r_ref, dec_ref, k_ref, v_ref, na_ref, b_ref, g_ref, *, seq, tm):
    i = pl.program_id(0)
    gn = gn_ref[...]
    h = _rms(x_ref[...], gn)
    h_last = _rms(xh_ref[SUBLANES - 1:SUBLANES, :], gn)
    h_last = jnp.where((i * tm) % seq == 0, 0.0, h_last)
    rowi = lax.broadcasted_iota(jnp.int32, (tm, 1), 0)
    h_prev = jnp.where(rowi == 0, h_last, pltpu.roll(h, 1, 0))
    xx = h_prev - h
    mu = mu_ref[...]
    xs = [h + xx * mu[n:n + 1, :] for n in range(6)]
    r = _bdot(xs[0], wrkv_ref[0])
    k = _bdot(xs[1], wrkv_ref[1])
    v = _bdot(xs[2], wrkv_ref[2])
    wl = w0_ref[...] + _bdot(jnp.tanh(_bdot(xs[3], w1_ref[...])), w2_ref[...])
    w = -jax.nn.softplus(-wl) - 0.5
    a = jax.nn.sigmoid(a0_ref[...] + _bdot(_bdot(xs[4], a1_ref[...]), a2_ref[...]))
    g = _bdot(jax.nn.sigmoid(_bdot(xs[5], g1_ref[...])), g2_ref[...])
    kk = k * kk_ref[...]
    norm = jnp.sqrt(_head_sum(kk * kk, hsum_ref, hexp_ref))
    kk = kk / jnp.maximum(norm, 1e-12)
    r_ref[...] = r
    dec_ref[...] = jnp.exp(-jnp.exp(w))
    k_ref[...] = k * (1.0 + (a - 1.0) * ka_ref[...])
    v_ref[...] = v
    na_ref[...] = -kk
    b_ref[...] = kk * a
    g_ref[...] = g


def _scan_kernel(r_ref, w_ref, k_ref, v_ref, a_ref, b_ref, y_ref, s_scr):
    n = s_scr.shape[0]

    @pl.when(pl.program_id(0) == 0)
    def _():
        s_scr[...] = jnp.zeros_like(s_scr)

    def step(t, c):
        v = v_ref[t]
        sa = jnp.zeros_like(v)
        for j in range(n):
            sa = sa + s_scr[j] * a_ref[t, j:j + 1, :]
        y = jnp.zeros_like(v)
        for j in range(n):
            s_new = (s_scr[j] * w_ref[t, j:j + 1, :] + sa * b_ref[t, j:j + 1, :]
                     + v * k_ref[t, j:j + 1, :])
            s_scr[j] = s_new
            y = y + s_new * r_ref[t, j:j + 1, :]
        y_ref[t] = y
        return c

    lax.fori_loop(0, r_ref.shape[0], step, 0)


def _rwkv_out_kernel(x_ref, y_ref, r_ref, k_ref, v_ref, g_ref, rk_ref, lng_ref, lnb_ref,
                     wo_ref, hsum_ref, hexp_ref, o_ref):
    y = y_ref[...]
    inv_n = 1.0 / HEAD_DIM
    mean = _head_sum(y, hsum_ref, hexp_ref) * inv_n
    yc = y - mean
    var = _head_sum(yc * yc, hsum_ref, hexp_ref) * inv_n
    yn = yc * lax.rsqrt(var + GN_EPS) * lng_ref[...] + lnb_ref[...]
    bonus = _head_sum(r_ref[...] * k_ref[...] * rk_ref[...], hsum_ref, hexp_ref) * v_ref[...]
    out = (yn + bonus) * g_ref[...]
    o_ref[...] = x_ref[...] + _bdot(out, wo_ref[...])


def _rwkv_mixer(x, gn, mu, w_rkv, w0, w1, w2, a0, a1, a2, g1, g2, k_k, k_a, r_k, ln_g, ln_b, w_o,
                batch, seq):
    t, d = x.shape
    heads = d // HEAD_DIM
    tm = min(TM_RWKV, seq)
    hb = tm // SUBLANES
    head_of = jnp.arange(d, dtype=jnp.int32) // HEAD_DIM
    hsum = (head_of[:, None] == jnp.arange(LANES, dtype=jnp.int32)[None, :]).astype(BF16)
    hexp = hsum.T
    const = lambda *shape: pl.BlockSpec(shape, lambda i: (0,) * len(shape))
    tok = pl.BlockSpec((tm, d), lambda i: (i, 0))
    vec = lambda a: a.reshape(1, d)
    r, dec, k, v, na, b, g = pl.pallas_call(
        functools.partial(_rwkv_proj_kernel, seq=seq, tm=tm),
        out_shape=tuple(jax.ShapeDtypeStruct((t, d), F32) for _ in range(7)),
        grid=(t // tm,),
        in_specs=[
            tok,
            pl.BlockSpec((SUBLANES, d), lambda i: (jnp.maximum(i * hb - 1, 0), 0)),
            const(1, d), const(6, d), const(3, d, d),
            const(1, d), const(d, w1.shape[1]), const(w2.shape[0], d),
            const(1, d), const(d, a1.shape[1]), const(a2.shape[0], d),
            const(d, g1.shape[1]), const(g2.shape[0], d),
            const(1, d), const(1, d), const(d, LANES), const(LANES, d),
        ],
        out_specs=tuple(tok for _ in range(7)),
        compiler_params=_cparams("parallel"),
        name="rwkv_proj",
    )(x, x, vec(gn), mu, w_rkv, vec(w0), w1, w2, vec(a0), a1, a2, g1, g2, vec(k_k), vec(k_a), hsum, hexp)

    def to_scan(z):
        return jnp.transpose(z.reshape(batch, seq, heads, HEAD_DIM), (1, 3, 0, 2)).reshape(
            seq, HEAD_DIM, batch * heads)

    bh = batch * heads
    tt = min(TT_SCAN, seq)
    blk = pl.BlockSpec((tt, HEAD_DIM, bh), lambda i: (i, 0, 0))
    y = pl.pallas_call(
        _scan_kernel,
        out_shape=jax.ShapeDtypeStruct((seq, HEAD_DIM, bh), F32),
        grid=(seq // tt,),
        in_specs=[blk] * 6,
        out_specs=blk,
        scratch_shapes=[pltpu.VMEM((HEAD_DIM, HEAD_DIM, bh), F32)],
        compiler_params=_cparams("arbitrary"),
        name="rwkv_scan",
    )(to_scan(r), to_scan(dec), to_scan(k), to_scan(v), to_scan(na), to_scan(b))
    y = jnp.transpose(y.reshape(seq, HEAD_DIM, batch, heads), (2, 0, 3, 1)).reshape(t, d)

    return pl.pallas_call(
        _rwkv_out_kernel,
        out_shape=jax.ShapeDtypeStruct((t, d), F32),
        grid=(t // tm,),
        in_specs=[tok] * 6 + [const(1, d), const(1, d), const(1, d), const(d, d),
                              const(d, LANES), const(LANES, d)],
        out_specs=tok,
        compiler_params=_cparams("parallel"),
        name="rwkv_out",
    )(x, y, r, k, v, g, r_k.reshape(1, d), vec(ln_g), vec(ln_b), w_o, hsum, hexp)


def kernel(x, norm_mix, norm_ffn, norm_final, a_w_in, a_ln_g, a_ln_b, a_w_s, a_b_s, a_w_out, b_w_in, b_w_grp, b_scale, b_w_out, c_mu, c_w_rkv, c_w0, c_w1, c_w2, c_a0, c_a1, c_a2, c_g1, c_g2, c_k_k, c_k_a, c_r_k, c_ln_g, c_ln_b, c_w_o, f_w_gate, f_w_up, f_w_down, m_router, m_w_gate, m_w_up, m_w_down):
    batch, seq, d = x.shape
    depth = norm_mix.shape[0]
    bf = lambda w: w.astype(BF16)
    xt = x.reshape(batch * seq, d)
    f_w = (bf(f_w_gate), bf(f_w_up), bf(f_w_down))
    m_w = (bf(m_w_gate), bf(m_w_up), bf(m_w_down))
    for layer in range(depth):
        kind, j = layer % 3, layer // 3
        if kind == 0:
            xt = _gmlp_mixer(xt, norm_mix[layer], bf(a_w_in[j]), a_ln_g[j], a_ln_b[j], a_w_s[j], a_b_s[j],
                             bf(a_w_out[j]))
        elif kind == 1:
            xt = _pool_mixer(xt, norm_mix[layer], bf(b_w_in[j]), bf(b_w_grp[j]), b_scale[j], bf(b_w_out[j]), seq)
        else:
            xt = _rwkv_mixer(xt, norm_mix[layer], c_mu[j], bf(c_w_rkv[j]), c_w0[j], bf(c_w1[j]), bf(c_w2[j]),
                             c_a0[j], bf(c_a1[j]), bf(c_a2[j]), bf(c_g1[j]), bf(c_g2[j]),
                             c_k_k[j], c_k_a[j], c_r_k[j], c_ln_g[j], c_ln_b[j], bf(c_w_o[j]), batch, seq)
        j = layer // 2
        last = layer == depth - 1
        if layer % 2 == 0:
            xt = _ffn_dense(xt, norm_ffn[layer], *f_w, j)
            if last:
                xt = _final_norm(xt, norm_final)
        else:
            xt = _moe_layer(xt, norm_ffn[layer], m_router[j], *m_w, j, norm_final, last)
    return xt.reshape(batch, seq, d)


def _final_norm_kernel(x_ref, gn_ref, o_ref):
    o_ref[...] = _rms(x_ref[...], gn_ref[...])


def _final_norm(x, gn):
    t, d = x.shape
    tm = min(TM_FFN, t)
    return pl.pallas_call(
        _final_norm_kernel,
        out_shape=jax.ShapeDtypeStruct((t, d), F32),
        grid=(t // tm,),
        in_specs=[pl.BlockSpec((tm, d), lambda i: (i, 0)), pl.BlockSpec((1, d), lambda i: (0, 0))],
        out_specs=pl.BlockSpec((tm, d), lambda i: (i, 0)),
        compiler_params=_cparams("parallel"),
        name="final_norm",
    )(x, gn.reshape(1, d))
```

```python
import functools

import jax
import jax.numpy as jnp
from jax import lax
from jax.experimental import pallas as pl
from jax.experimental.pallas import tpu as pltpu

F32 = jnp.float32
BF16 = jnp.bfloat16

RMS_EPS = 1e-6
LN_EPS = 1e-5
GN_EPS = 64e-5

CHUNK = 128
A_GROUPS = 8
POOL_WINDOWS = (2, 4, 8, 16)
POOL_HALO = 16
HEAD_DIM = 64
N_EXPERTS = 8
TOP_K = 2

LANES = 128
SUBLANES = 8
VMEM_LIMIT = 56 * 1024 * 1024

TM_GMLP = 512
TM_FFN = 1024
TF_FFN = 512
TM_POOL = 512
TM_ROUTER = 512
TB_MOE = 512
TM_MOE = 512
DMA_UNROLL = 8
TT_RWKV = 32
TT_SCAN = 64
SCAN_J_UNROLL = 8


def _cparams(*sem):
    return pltpu.CompilerParams(dimension_semantics=sem, vmem_limit_bytes=VMEM_LIMIT)


def _rms(x, g):
    return x * lax.rsqrt(jnp.mean(x * x, axis=-1, keepdims=True) + RMS_EPS) * g


def _bdot(a, b):
    return jnp.dot(a.astype(BF16), b.astype(BF16), preferred_element_type=F32)


def _split_dot(a, b):
    hi = a.astype(BF16)
    lo = (a - hi.astype(F32)).astype(BF16)
    return (jnp.dot(hi, b, preferred_element_type=F32)
            + jnp.dot(lo, b, preferred_element_type=F32))


def _gmlp_kernel(x_ref, gn_ref, win_ref, lng_ref, lnb_ref, ws_ref, bs_ref, wout_ref,
                 o_ref, us_scr, *, width, n_chunks):
    x = x_ref[...]
    h = _rms(x, gn_ref[...])
    z = _bdot(h, win_ref[...])
    z = 0.5 * z * (1.0 + lax.erf(z * (2.0 ** -0.5)))
    u = z[:, :width]
    v = z[:, width:]
    mu = jnp.mean(v, axis=-1, keepdims=True)
    vc = v - mu
    var = jnp.mean(vc * vc, axis=-1, keepdims=True)
    vb = (vc * lax.rsqrt(var + LN_EPS) * lng_ref[...] + lnb_ref[...]).astype(BF16)
    gd = width // A_GROUPS
    row = lax.broadcasted_iota(jnp.int32, (CHUNK, CHUNK), 0)
    col = lax.broadcasted_iota(jnp.int32, (CHUNK, CHUNK), 1)
    causal = row >= col
    for g in range(A_GROUPS):
        w = jnp.where(causal, ws_ref[g], 0.0).astype(BF16)
        rhs = jnp.concatenate(
            [vb[c * CHUNK:(c + 1) * CHUNK, g * gd:(g + 1) * gd] for c in range(n_chunks)], axis=1)
        s = jnp.dot(w, rhs, preferred_element_type=F32)
        bias = bs_ref[g]
        for c in range(n_chunks):
            sc = s[:, c * gd:(c + 1) * gd] + bias
            uc = u[c * CHUNK:(c + 1) * CHUNK, g * gd:(g + 1) * gd]
            us_scr[c * CHUNK:(c + 1) * CHUNK, g * gd:(g + 1) * gd] = (uc * sc).astype(BF16)
    o_ref[...] = x + jnp.dot(us_scr[...], wout_ref[...].astype(BF16), preferred_element_type=F32)


def _gmlp_mixer(x, gn, w_in, ln_g, ln_b, w_s, b_s, w_out):
    t, d = x.shape
    width = w_in.shape[1] // 2
    gd = width // A_GROUPS
    tm = min(TM_GMLP, t)
    n_chunks = tm // CHUNK
    bias = jnp.broadcast_to(b_s[:, :, None], (A_GROUPS, CHUNK, gd))
    const = lambda *shape: pl.BlockSpec(shape, lambda i: (0,) * len(shape))
    return pl.pallas_call(
        functools.partial(_gmlp_kernel, width=width, n_chunks=n_chunks),
        out_shape=jax.ShapeDtypeStruct((t, d), F32),
        grid=(t // tm,),
        in_specs=[
            pl.BlockSpec((tm, d), lambda i: (i, 0)),
            const(1, d), const(d, 2 * width), const(1, width), const(1, width),
            const(A_GROUPS, CHUNK, CHUNK), const(A_GROUPS, CHUNK, gd), const(width, d),
        ],
        out_specs=pl.BlockSpec((tm, d), lambda i: (i, 0)),
        scratch_shapes=[pltpu.VMEM((tm, width), BF16)],
        compiler_params=_cparams("parallel"),
        name="gmlp_mixer",
    )(x, gn.reshape(1, d), w_in, ln_g.reshape(1, width), ln_b.reshape(1, width), w_s, bias, w_out)


def _pool_kernel(x_ref, xh_ref, gn_ref, win_ref, wgrp_ref, scale_ref, wout_ref, o_ref, d_scr,
                 *, seq, tm):
    i = pl.program_id(0)
    x = x_ref[...]
    width = win_ref.shape[1]
    gd = width // len(POOL_WINDOWS)
    t0 = (i * tm) % seq
    xa = jnp.concatenate([xh_ref[...], x], axis=0)
    p_all = _bdot(_rms(xa, gn_ref[...]), win_ref[...])
    r = lax.broadcasted_iota(jnp.int32, (POOL_HALO + tm, 1), 0)
    p_all = jnp.where(r + (t0 - POOL_HALO) >= 0, p_all, 0.0)
    pos = lax.broadcasted_iota(jnp.int32, (tm, 1), 0) + (t0 + 1)

    sums = p_all
    have = 1
    for gi, win in enumerate(POOL_WINDOWS):
        while have < win:
            sums = sums[have:] + sums[:-have]
            have *= 2
        lo, hi = gi * gd, (gi + 1) * gd
        s = sums[POOL_HALO - (win - 1):POOL_HALO - (win - 1) + tm, lo:hi]
        count = jnp.minimum(pos, win).astype(F32)
        d_scr[:, lo:hi] = (s / count - p_all[POOL_HALO:, lo:hi]).astype(BF16)
    ys = []
    for gi in range(len(POOL_WINDOWS)):
        lo, hi = gi * gd, (gi + 1) * gd
        ys.append(jnp.dot(d_scr[:, lo:hi], wgrp_ref[gi].astype(BF16), preferred_element_type=F32))
    y = jnp.concatenate(ys, axis=1) * scale_ref[...]
    o_ref[...] = x + _bdot(y, wout_ref[...])


def _pool_mixer(x, gn, w_in, w_grp, scale, w_out, seq):
    t, d = x.shape
    width = w_in.shape[1]
    ng = len(POOL_WINDOWS)
    gd = width // ng
    tm = min(TM_POOL, seq)
    hb = tm // POOL_HALO
    const = lambda *shape: pl.BlockSpec(shape, lambda i: (0,) * len(shape))
    return pl.pallas_call(
        functools.partial(_pool_kernel, seq=seq, tm=tm),
        out_shape=jax.ShapeDtypeStruct((t, d), F32),
        grid=(t // tm,),
        in_specs=[
            pl.BlockSpec((tm, d), lambda i: (i, 0)),
            pl.BlockSpec((POOL_HALO, d), lambda i: (jnp.maximum(i * hb - 1, 0), 0)),
            const(1, d), const(d, width), const(ng, gd, gd), const(1, width), const(width, d),
        ],
        out_specs=pl.BlockSpec((tm, d), lambda i: (i, 0)),
        scratch_shapes=[pltpu.VMEM((tm, width), BF16)],
        compiler_params=_cparams("parallel"),
        name="pool_mixer",
    )(x, x, gn.reshape(1, d), w_in, w_grp, scale.reshape(1, width), w_out)


def _ffn_kernel(x_ref, gn_ref, wg_ref, wu_ref, wd_ref, o_ref, h_scr, acc_scr):
    j = pl.program_id(1)

    @pl.when(j == 0)
    def _():
        h_scr[...] = _rms(x_ref[...], gn_ref[...]).astype(BF16)
        acc_scr[...] = jnp.zeros_like(acc_scr)

    h = h_scr[...]
    gate = jnp.dot(h, wg_ref[...].astype(BF16), preferred_element_type=F32)
    up = jnp.dot(h, wu_ref[...].astype(BF16), preferred_element_type=F32)
    act = (gate * jax.nn.sigmoid(gate) * up).astype(BF16)
    acc_scr[...] += jnp.dot(act, wd_ref[...].astype(BF16), preferred_element_type=F32)

    @pl.when(j == pl.num_programs(1) - 1)
    def _():
        o_ref[...] = x_ref[...] + acc_scr[...]


def _ffn_dense(x, gn, w_gate, w_up, w_down, layer):
    t, d = x.shape
    f = w_gate.shape[2]
    tm = min(TM_FFN, t)
    tf = min(TF_FFN, f)
    return pl.pallas_call(
        _ffn_kernel,
        out_shape=jax.ShapeDtypeStruct((t, d), F32),
        grid=(t // tm, f // tf),
        in_specs=[
            pl.BlockSpec((tm, d), lambda i, j: (i, 0)),
            pl.BlockSpec((1, d), lambda i, j: (0, 0)),
            pl.BlockSpec((None, d, tf), lambda i, j: (layer, 0, j)),
            pl.BlockSpec((None, d, tf), lambda i, j: (layer, 0, j)),
            pl.BlockSpec((None, tf, d), lambda i, j: (layer, j, 0)),
        ],
        out_specs=pl.BlockSpec((tm, d), lambda i, j: (i, 0)),
        scratch_shapes=[pltpu.VMEM((tm, d), BF16), pltpu.VMEM((tm, d), F32)],
        compiler_params=_cparams("parallel", "arbitrary"),
        name="ffn_dense",
    )(x, gn.reshape(1, d), w_gate, w_up, w_down)


def _router_kernel(x_ref, gn_ref, wr_ref, idx_ref, gate_ref):
    h = _rms(x_ref[...], gn_ref[...])
    logits = jnp.dot(h, wr_ref[...], preferred_element_type=F32, precision=lax.Precision.HIGHEST)
    lane = lax.broadcasted_iota(jnp.int32, logits.shape, 1)
    lane_f = lane.astype(F32)
    neg = jnp.float32(-jnp.inf)
    logits = jnp.where(lane < N_EXPERTS, logits, neg)
    m1 = jnp.max(logits, axis=-1, keepdims=True)
    i1 = jnp.min(jnp.where(logits == m1, lane_f, float(LANES)), axis=-1, keepdims=True)
    rest = jnp.where(lane_f == i1, neg, logits)
    m2 = jnp.max(rest, axis=-1, keepdims=True)
    i2 = jnp.min(jnp.where(rest == m2, lane_f, float(LANES)), axis=-1, keepdims=True)
    e = jnp.exp(m2 - m1)
    den = 1.0 + e
    idx_ref[...] = jnp.where(lane == 0, i1, jnp.where(lane == 1, i2, 0.0)).astype(jnp.int32)
    gate_ref[...] = jnp.where(lane == 0, 1.0 / den, jnp.where(lane == 1, e / den, 0.0))


def _moe_router(x, gn, router):
    t, d = x.shape
    tm = min(TM_ROUTER, t)
    wr = jnp.zeros((d, LANES), F32).at[:, :N_EXPERTS].set(router)
    return pl.pallas_call(
        _router_kernel,
        out_shape=(jax.ShapeDtypeStruct((t, LANES), jnp.int32), jax.ShapeDtypeStruct((t, LANES), F32)),
        grid=(t // tm,),
        in_specs=[
            pl.BlockSpec((tm, d), lambda i: (i, 0)),
            pl.BlockSpec((1, d), lambda i: (0, 0)),
            pl.BlockSpec((d, LANES), lambda i: (0, 0)),
        ],
        out_specs=(pl.BlockSpec((tm, LANES), lambda i: (i, 0)), pl.BlockSpec((tm, LANES), lambda i: (i, 0))),
        compiler_params=_cparams("parallel"),
        name="moe_router",
    )(x, gn.reshape(1, d), wr)


def _tok_copy(src, src_tok, dst, dst_tok, sem):
    return pltpu.make_async_copy(
        src.at[pl.ds(pl.multiple_of(src_tok * SUBLANES, SUBLANES), SUBLANES)],
        dst.at[pl.ds(pl.multiple_of(dst_tok * SUBLANES, SUBLANES), SUBLANES)], sem)


def _to_tiles(ref, base, val):
    n, d = val.shape
    for c in range(d // LANES):
        ref[pl.ds(base + c, n, stride=SUBLANES), :] = val[:, c * LANES:(c + 1) * LANES]


def _from_tiles(ref, base, n, d):
    return jnp.concatenate(
        [ref[pl.ds(base + c, n, stride=SUBLANES), :] for c in range(d // LANES)], axis=1)


def _dispatch_kernel(dest_ref, x_ref, gn_ref, buf_in_ref, buf_ref, h_scr, sem):
    del buf_in_ref
    n = x_ref.shape[0]
    _to_tiles(h_scr, 0, _rms(x_ref[...], gn_ref[...]))

    def copies(r):
        return [_tok_copy(h_scr, r, buf_ref, dest_ref[TOP_K * r + k], sem) for k in range(TOP_K)]

    def start(g, c):
        for u in range(DMA_UNROLL):
            for k, cp in enumerate(copies(g * DMA_UNROLL + u)):
                cp.start(priority=k % 2)
        return c

    lax.fori_loop(0, n // DMA_UNROLL, start, 0)

    def wait(g, c):
        for u in range(DMA_UNROLL):
            for cp in copies(g * DMA_UNROLL + u):
                cp.wait()
        return c

    lax.fori_loop(0, n // DMA_UNROLL, wait, 0)


def _moe_dispatch(x, gn, dest, n_rows):
    t, d = x.shape
    tb = min(TB_MOE, t)
    tile_rows = d // LANES
    assert tile_rows == SUBLANES
    buf0 = jnp.zeros((n_rows * tile_rows, LANES), F32)
    return pl.pallas_call(
        _dispatch_kernel,
        out_shape=jax.ShapeDtypeStruct((n_rows * tile_rows, LANES), F32),
        grid=(t // tb,),
        in_specs=[
            pl.BlockSpec((TOP_K * tb,), lambda i: (i,), memory_space=pltpu.SMEM),
            pl.BlockSpec((tb, d), lambda i: (i, 0)),
            pl.BlockSpec((1, d), lambda i: (0, 0)),
            pl.BlockSpec(memory_space=pl.ANY),
        ],
        out_specs=pl.BlockSpec(memory_space=pl.ANY),
        scratch_shapes=[pltpu.VMEM((tb * tile_rows, LANES), F32), pltpu.SemaphoreType.DMA],
        input_output_aliases={3: 0},
        compiler_params=_cparams("arbitrary"),
        name="moe_dispatch",
    )(dest, x, gn.reshape(1, d), buf0)


def _moe_ffn_kernel(be_ref, nu_ref, xg_ref, wg_ref, wu_ref, wd_ref, o_ref, h_scr, acc_scr):
    i = pl.program_id(0)
    j = pl.program_id(1)
    used = i < nu_ref[0]
    tm, d = h_scr.shape

    @pl.when(j == 0)
    def _():
        h_scr[...] = _from_tiles(xg_ref, 0, tm, d).astype(BF16)
        acc_scr[...] = jnp.zeros_like(acc_scr)

    @pl.when(used)
    def _():
        h = h_scr[...]
        gate = jnp.dot(h, wg_ref[...].astype(BF16), preferred_element_type=F32)
        up = jnp.dot(h, wu_ref[...].astype(BF16), preferred_element_type=F32)
        act = (gate * jax.nn.sigmoid(gate) * up).astype(BF16)
        acc_scr[...] += jnp.dot(act, wd_ref[...].astype(BF16), preferred_element_type=F32)

    @pl.when(j == pl.num_programs(1) - 1)
    def _():
        _to_tiles(o_ref, 0, acc_scr[...])


def _moe_ffn(buf, block_expert, n_used, w_gate, w_up, w_down, layer, d):
    n_rows = buf.shape[0] // SUBLANES
    f = w_gate.shape[3]
    tm = TM_MOE
    tf = min(TF_FFN, f)
    nb, nf = n_rows // tm, f // tf

    def fj(i, j, nu):
        return jnp.where(i < nu[0], j, nf - 1)

    def row(i, nu):
        return jnp.minimum(i, nu[0] - 1)

    grid_spec = pltpu.PrefetchScalarGridSpec(
        num_scalar_prefetch=2,
        grid=(nb, nf),
        in_specs=[
            pl.BlockSpec((tm * SUBLANES, LANES), lambda i, j, be, nu: (row(i, nu), 0)),
            pl.BlockSpec((None, None, d, tf), lambda i, j, be, nu: (layer, be[i], 0, fj(i, j, nu))),
            pl.BlockSpec((None, None, d, tf), lambda i, j, be, nu: (layer, be[i], 0, fj(i, j, nu))),
            pl.BlockSpec((None, None, tf, d), lambda i, j, be, nu: (layer, be[i], fj(i, j, nu), 0)),
        ],
        out_specs=pl.BlockSpec((tm * SUBLANES, LANES), lambda i, j, be, nu: (i, 0)),
        scratch_shapes=[pltpu.VMEM((tm, d), BF16), pltpu.VMEM((tm, d), F32)],
    )
    return pl.pallas_call(
        _moe_ffn_kernel,
        out_shape=jax.ShapeDtypeStruct(buf.shape, F32),
        grid_spec=grid_spec,
        compiler_params=_cparams("arbitrary", "arbitrary"),
        name="moe_ffn",
    )(block_expert, n_used, buf, w_gate, w_up, w_down)


def _combine_kernel(dest_ref, x_ref, gate_ref, gn_ref, y_ref, o_ref, rows_scr, sem, *, final_norm):
    n, d = x_ref.shape

    def copies(r):
        return [_tok_copy(y_ref, dest_ref[TOP_K * r + k], rows_scr, k * n + r, sem) for k in range(TOP_K)]

    def start(g, c):
        for u in range(DMA_UNROLL):
            for k, cp in enumerate(copies(g * DMA_UNROLL + u)):
                cp.start(priority=k % 2)
        return c

    lax.fori_loop(0, n // DMA_UNROLL, start, 0)

    def wait(g, c):
        for u in range(DMA_UNROLL):
            for cp in copies(g * DMA_UNROLL + u):
                cp.wait()
        return c

    lax.fori_loop(0, n // DMA_UNROLL, wait, 0)
    gates = gate_ref[...]
    out = x_ref[...]
    for k in range(TOP_K):
        out = out + _from_tiles(rows_scr, k * n * SUBLANES, n, d) * gates[:, k:k + 1]
    if final_norm:
        out = _rms(out, gn_ref[...])
    o_ref[...] = out


def _moe_combine(x, y_grouped, dest, gates, gn_final, final_norm):
    t, d = x.shape
    tb = min(TB_MOE, t)
    return pl.pallas_call(
        functools.partial(_combine_kernel, final_norm=final_norm),
        out_shape=jax.ShapeDtypeStruct((t, d), F32),
        grid=(t // tb,),
        in_specs=[
            pl.BlockSpec((TOP_K * tb,), lambda i: (i,), memory_space=pltpu.SMEM),
            pl.BlockSpec((tb, d), lambda i: (i, 0)),
            pl.BlockSpec((tb, LANES), lambda i: (i, 0)),
            pl.BlockSpec((1, d), lambda i: (0, 0)),
            pl.BlockSpec(memory_space=pl.ANY),
        ],
        out_specs=pl.BlockSpec((tb, d), lambda i: (i, 0)),
        scratch_shapes=[pltpu.VMEM((TOP_K * tb * SUBLANES, LANES), F32), pltpu.SemaphoreType.DMA],
        compiler_params=_cparams("arbitrary"),
        name="moe_combine",
    )(dest, x, gates, gn_final.reshape(1, d), y_grouped)


def _moe_layer(x, gn, router, w_gate, w_up, w_down, layer, gn_final, final_norm):
    t, d = x.shape
    tm = TM_MOE
    idx, gates = _moe_router(x, gn, router)
    flat_e = idx[:, :TOP_K].reshape(-1)
    onehot = (flat_e[:, None] == jnp.arange(N_EXPERTS, dtype=jnp.int32)[None, :]).astype(jnp.int32)
    csum = jnp.cumsum(onehot, axis=0)
    rank = jnp.sum(csum * onehot, axis=1) - 1
    counts = csum[-1]
    padded = (counts + tm - 1) // tm * tm
    pad_end = jnp.cumsum(padded)
    pad_start = pad_end - padded
    dest = (jnp.sum(pad_start[None, :] * onehot, axis=1) + rank).astype(jnp.int32)
    nb = -(-(t * TOP_K) // tm) + N_EXPERTS
    block_start = jnp.arange(nb, dtype=jnp.int32) * tm
    block_expert = jnp.minimum(
        jnp.sum((block_start[:, None] >= pad_end[None, :]).astype(jnp.int32), axis=1), N_EXPERTS - 1)
    n_used = (pad_end[-1] // tm).astype(jnp.int32).reshape(1)
    buf = _moe_dispatch(x, gn, dest, nb * tm)
    y_grouped = _moe_ffn(buf, block_expert.astype(jnp.int32), n_used, w_gate, w_up, w_down, layer, d)
    return _moe_combine(x, y_grouped, dest, gates, gn_final, final_norm)


def _head_sum(a, hsum_ref, hexp_ref):
    return _split_dot(_split_dot(a, hsum_ref[...]), hexp_ref[...])


def _rwkv_proj_kernel(x_ref, xh_ref, gn_ref, mu_ref, wrkv_ref, w0_ref, w1_ref, w2_ref,
                      a0_ref, a1_ref, a2_ref, g1_ref, g2_ref, kk_ref, ka_ref, hsum_ref, hexp_ref,
                      r_ref, dec_ref, k_ref, v_ref, na_ref, b_ref, g_ref, x_scr):
    i = pl.program_id(0)
    nb, tt, d = x_ref.shape
    n_blk = d // LANES
    for b in range(nb):
        for c in range(n_blk):
            x_scr[c, pl.ds(b, tt, stride=nb), :] = x_ref[b, :, c * LANES:(c + 1) * LANES]
    gn = gn_ref[...]
    h = _rms(jnp.concatenate([x_scr[c] for c in range(n_blk)], axis=1), gn)
    x_last = jnp.concatenate([xh_ref[b, SUBLANES - 1:SUBLANES, :] for b in range(nb)], axis=0)
    h_last = jnp.where(i == 0, 0.0, _rms(x_last, gn))
    h_prev = jnp.concatenate([h_last, h[:-nb]], axis=0)
    xx = h_prev - h
    mu = mu_ref[...]
    xs = [h + xx * mu[n:n + 1, :] for n in range(6)]
    r = _bdot(xs[0], wrkv_ref[0])
    k = _bdot(xs[1], wrkv_ref[1])
    v = _bdot(xs[2], wrkv_ref[2])
    wl = w0_ref[...] + _bdot(jnp.tanh(_bdot(xs[3], w1_ref[...])), w2_ref[...])
    w = -jax.nn.softplus(-wl) - 0.5
    a = jax.nn.sigmoid(a0_ref[...] + _bdot(_bdot(xs[4], a1_ref[...]), a2_ref[...]))
    g = _bdot(jax.nn.sigmoid(_bdot(xs[5], g1_ref[...])), g2_ref[...])
    kk = k * kk_ref[...]
    norm = jnp.sqrt(_head_sum(kk * kk, hsum_ref, hexp_ref))
    kk = kk / jnp.maximum(norm, 1e-12)
    r_ref[...] = r
    dec_ref[...] = jnp.exp(-jnp.exp(w))
    k_ref[...] = k * (1.0 + (a - 1.0) * ka_ref[...])
    v_ref[...] = v
    na_ref[...] = -kk
    b_ref[...] = kk * a
    g_ref[...] = g


def _half_swap(a0, a1):
    lo = lax.broadcasted_iota(jnp.int32, a0.shape, 1) < LANES // 2
    return (jnp.where(lo, a0, pltpu.roll(a1, LANES // 2, 1)),
            jnp.where(lo, pltpu.roll(a0, LANES // 2, 1), a1))


def _scan_load_pair(ref, row0, nb):
    p0 = ref[pl.ds(row0, nb), :]
    p1 = ref[pl.ds(row0 + nb, nb), :]
    n_blk = p0.shape[1] // LANES
    even, odd = [], []
    for c in range(n_blk):
        e, o = _half_swap(p0[:, c * LANES:(c + 1) * LANES], p1[:, c * LANES:(c + 1) * LANES])
        even.append(e)
        odd.append(o)
    return jnp.concatenate(even + odd, axis=0).T


def _scan_store_pair(ref, row0, nb, y_pair):
    yt = y_pair.T
    n_blk = ref.shape[1] // LANES
    for c in range(n_blk):
        y0, y1 = _half_swap(yt[c * nb:(c + 1) * nb], yt[(n_blk + c) * nb:(n_blk + c + 1) * nb])
        ref[pl.ds(row0, nb), c * LANES:(c + 1) * LANES] = y0
        ref[pl.ds(row0 + nb, nb), c * LANES:(c + 1) * LANES] = y1


def _scan_kernel(r_ref, w_ref, k_ref, v_ref, a_ref, b_ref, y_ref, s_scr, t_scr, *, nb):
    n = s_scr.shape[0]
    refs = (r_ref, w_ref, k_ref, v_ref, a_ref, b_ref)
    R, W, K, V, A, B = range(6)
    n_pairs = r_ref.shape[0] // (2 * nb)

    @pl.when(pl.program_id(0) == 0)
    def _():
        s_scr[...] = jnp.zeros_like(s_scr)

    def load_pair(p, slot, which):
        row0 = pl.multiple_of(p * 2 * nb, 2 * nb)
        for q in which:
            t_scr[slot, q] = _scan_load_pair(refs[q], row0, nb)

    load_pair(0, 0, range(6))

    def pair(p, slot):
        p_next = jnp.minimum(p + 1, n_pairs - 1)
        ys = []
        for t2 in range(2):
            base = t2 * n
            load_pair(p_next, 1 - slot, range(3 * t2, 3 * t2 + 3))
            v = t_scr[slot, V, base:base + n, :]
            sa = jnp.zeros_like(v)
            for j in range(n):
                sa = sa + s_scr[j] * t_scr[slot, A, base + j:base + j + 1, :]

            def row(q, j):
                return t_scr[slot, q, pl.ds(base + j, 1), :]

            def pass2(jc, y):
                for u in range(SCAN_J_UNROLL):
                    j = jc * SCAN_J_UNROLL + u
                    s_new = s_scr[j] * row(W, j) + sa * row(B, j) + v * row(K, j)
                    s_scr[j] = s_new
                    y = y + s_new * row(R, j)
                return y

            ys.append(lax.fori_loop(0, n // SCAN_J_UNROLL, pass2, jnp.zeros_like(v)))
        _scan_store_pair(y_ref, pl.multiple_of(p * 2 * nb, 2 * nb), nb, jnp.concatenate(ys, axis=0))

    def two_pairs(pp, c):
        pair(2 * pp, 0)
        pair(2 * pp + 1, 1)
        return c

    lax.fori_loop(0, n_pairs // 2, two_pairs, 0)


def _rwkv_out_kernel(x_ref, y_ref, r_ref, k_ref, v_ref, g_ref, rk_ref, lng_ref, lnb_ref,
                     wo_ref, hsum_ref, hexp_ref, o_ref, m_scr):
    nb, tt, d = x_ref.shape
    n_blk = d // LANES
    y = y_ref[...]
    inv_n = 1.0 / HEAD_DIM
    mean = _head_sum(y, hsum_ref, hexp_ref) * inv_n
    yc = y - mean
    var = _head_sum(yc * yc, hsum_ref, hexp_ref) * inv_n
    yn = yc * lax.rsqrt(var + GN_EPS) * lng_ref[...] + lnb_ref[...]
    bonus = _head_sum(r_ref[...] * k_ref[...] * rk_ref[...], hsum_ref, hexp_ref) * v_ref[...]
    out = (yn + bonus) * g_ref[...]
    m = _bdot(out, wo_ref[...])
    for c in range(n_blk):
        m_scr[c] = m[:, c * LANES:(c + 1) * LANES]
    for b in range(nb):
        for c in range(n_blk):
            cs = slice(c * LANES, (c + 1) * LANES)
            o_ref[b, :, cs] = x_ref[b, :, cs] + m_scr[c, pl.ds(b, tt, stride=nb), :]


def _rwkv_mixer(x, gn, mu, w_rkv, w0, w1, w2, a0, a1, a2, g1, g2, k_k, k_a, r_k, ln_g, ln_b, w_o,
                batch, seq):
    t, d = x.shape
    heads = d // HEAD_DIM
    assert batch == SUBLANES and batch * heads == LANES
    tt = min(TT_RWKV, seq)
    head_of = jnp.arange(d, dtype=jnp.int32) // HEAD_DIM
    hsum = (head_of[:, None] == jnp.arange(LANES, dtype=jnp.int32)[None, :]).astype(BF16)
    hexp = hsum.T
    const = lambda *shape: pl.BlockSpec(shape, lambda i: (0,) * len(shape))
    x3 = x.reshape(batch, seq, d)
    xblk = pl.BlockSpec((batch, tt, d), lambda i: (0, i, 0))
    tok = pl.BlockSpec((tt * batch, d), lambda i: (i, 0))
    vec = lambda a: a.reshape(1, d)
    r, dec, k, v, na, b, g = pl.pallas_call(
        _rwkv_proj_kernel,
        out_shape=tuple(jax.ShapeDtypeStruct((t, d), F32) for _ in range(7)),
        grid=(seq // tt,),
        in_specs=[
            xblk,
            pl.BlockSpec((batch, SUBLANES, d), lambda i: (0, jnp.maximum(i * (tt // SUBLANES) - 1, 0), 0)),
            const(1, d), const(6, d), const(3, d, d),
            const(1, d), const(d, w1.shape[1]), const(w2.shape[0], d),
            const(1, d), const(d, a1.shape[1]), const(a2.shape[0], d),
            const(d, g1.shape[1]), const(g2.shape[0], d),
            const(1, d), const(1, d), const(d, LANES), const(LANES, d),
        ],
        out_specs=tuple(tok for _ in range(7)),
        scratch_shapes=[pltpu.VMEM((d // LANES, tt * batch, LANES), F32)],
        compiler_params=_cparams("parallel"),
        name="rwkv_proj",
    )(x3, x3, vec(gn), mu, w_rkv, vec(w0), w1, w2, vec(a0), a1, a2, g1, g2, vec(k_k), vec(k_a), hsum, hexp)

    ts = min(TT_SCAN, seq)
    blk = pl.BlockSpec((ts * batch, d), lambda i: (i, 0))
    y = pl.pallas_call(
        functools.partial(_scan_kernel, nb=batch),
        out_shape=jax.ShapeDtypeStruct((t, d), F32),
        grid=(seq // ts,),
        in_specs=[blk] * 6,
        out_specs=blk,
        scratch_shapes=[pltpu.VMEM((HEAD_DIM, HEAD_DIM, LANES), F32),
                        pltpu.VMEM((2, 6, 2 * HEAD_DIM, LANES), F32)],
        compiler_params=_cparams("arbitrary"),
        name="rwkv_scan",
    )(r, dec, k, v, na, b)

    out = pl.pallas_call(
        _rwkv_out_kernel,
        out_shape=jax.ShapeDtypeStruct((batch, seq, d), F32),
        grid=(seq // tt,),
        in_specs=[xblk] + [tok] * 5 + [const(1, d), const(1, d), const(1, d), const(d, d),
                                       const(d, LANES), const(LANES, d)],
        out_specs=xblk,
        scratch_shapes=[pltpu.VMEM((d // LANES, tt * batch, LANES), F32)],
        compiler_params=_cparams("parallel"),
        name="rwkv_out",
    )(x3, y, r, k, v, g, r_k.reshape(1, d), vec(ln_g), vec(ln_b), w_o, hsum, hexp)
    return out.reshape(t, d)


def kernel(x, norm_mix, norm_ffn, norm_final, a_w_in, a_ln_g, a_ln_b, a_w_s, a_b_s, a_w_out, b_w_in, b_w_grp, b_scale, b_w_out, c_mu, c_w_rkv, c_w0, c_w1, c_w2, c_a0, c_a1, c_a2, c_g1, c_g2, c_k_k, c_k_a, c_r_k, c_ln_g, c_ln_b, c_w_o, f_w_gate, f_w_up, f_w_down, m_router, m_w_gate, m_w_up, m_w_down):
    batch, seq, d = x.shape
    depth = norm_mix.shape[0]
    bf = lambda w: w.astype(BF16)
    xt = x.reshape(batch * seq, d)
    f_w = (bf(f_w_gate), bf(f_w_up), bf(f_w_down))
    m_w = (bf(m_w_gate), bf(m_w_up), bf(m_w_down))
    for layer in range(depth):
        kind, j = layer % 3, layer // 3
        if kind == 0:
            xt = _gmlp_mixer(xt, norm_mix[layer], bf(a_w_in[j]), a_ln_g[j], a_ln_b[j], a_w_s[j], a_b_s[j],
                             bf(a_w_out[j]))
        elif kind == 1:
            xt = _pool_mixer(xt, norm_mix[layer], bf(b_w_in[j]), bf(b_w_grp[j]), b_scale[j], bf(b_w_out[j]), seq)
        else:
            xt = _rwkv_mixer(xt, norm_mix[layer], c_mu[j], bf(c_w_rkv[j]), c_w0[j], bf(c_w1[j]), bf(c_w2[j]),
                             c_a0[j], bf(c_a1[j]), bf(c_a2[j]), bf(c_g1[j]), bf(c_g2[j]),
                             c_k_k[j], c_k_a[j], c_r_k[j], c_ln_g[j], c_ln_b[j], bf(c_w_o[j]), batch, seq)
        j = layer // 2
        last = layer == depth - 1
        if layer % 2 == 0:
            xt = _ffn_dense(xt, norm_ffn[layer], *f_w, j)
            if last:
                xt = _final_norm(xt, norm_final)
        else:
            xt = _moe_layer(xt, norm_ffn[layer], m_router[j], *m_w, j, norm_final, last)
    return xt.reshape(batch, seq, d)


def _final_norm_kernel(x_ref, gn_ref, o_ref):
    o_ref[...] = _rms(x_ref[...], gn_ref[...])


def _final_norm(x, gn):
    t, d = x.shape
    tm = min(TM_FFN, t)
    return pl.pallas_call(
        _final_norm_kernel,
        out_shape=jax.ShapeDtypeStruct((t, d), F32),
        grid=(t // tm,),
        in_specs=[pl.BlockSpec((tm, d), lambda i: (i, 0)), pl.BlockSpec((1, d), lambda i: (0, 0))],
        out_specs=pl.BlockSpec((tm, d), lambda i: (i, 0)),
        compiler_params=_cparams("parallel"),
        name="final_norm",
    )(x, gn.reshape(1, d))
```

```python
import functools

import jax
import jax.numpy as jnp
from jax import lax
from jax.experimental import pallas as pl
from jax.experimental.pallas import tpu as pltpu

F32 = jnp.float32
BF16 = jnp.bfloat16

RMS_EPS = 1e-6
LN_EPS = 1e-5
GN_EPS = 64e-5

CHUNK = 128
A_GROUPS = 8
POOL_WINDOWS = (2, 4, 8, 16)
POOL_HALO = 16
HEAD_DIM = 64
N_EXPERTS = 8
TOP_K = 2

LANES = 128
SUBLANES = 8
VMEM_LIMIT = 56 * 1024 * 1024

TM_GMLP = 512
TM_FFN = 1024
TF_FFN = 512
TM_POOL = 512
TM_ROUTER = 512
TB_MOE = 512
TM_MOE = 1024
MOE_SUB = 512
DMA_UNROLL = 8
TT_RWKV = 32
TT_SCAN = 64
SCAN_J_UNROLL = 8


def _cparams(*sem):
    return pltpu.CompilerParams(dimension_semantics=sem, vmem_limit_bytes=VMEM_LIMIT)


def _rms(x, g):
    return x * lax.rsqrt(jnp.mean(x * x, axis=-1, keepdims=True) + RMS_EPS) * g


def _bdot(a, b):
    return jnp.dot(a.astype(BF16), b.astype(BF16), preferred_element_type=F32)


def _split_dot(a, b):
    hi = a.astype(BF16)
    lo = (a - hi.astype(F32)).astype(BF16)
    return (jnp.dot(hi, b, preferred_element_type=F32)
            + jnp.dot(lo, b, preferred_element_type=F32))


def _gmlp_kernel(x_ref, gn_ref, win_ref, lng_ref, lnb_ref, ws_ref, bs_ref, wout_ref,
                 o_ref, us_scr, *, width, n_chunks):
    x = x_ref[...]
    h = _rms(x, gn_ref[...])
    z = _bdot(h, win_ref[...])
    z = 0.5 * z * (1.0 + lax.erf(z * (2.0 ** -0.5)))
    u = z[:, :width]
    v = z[:, width:]
    mu = jnp.mean(v, axis=-1, keepdims=True)
    vc = v - mu
    var = jnp.mean(vc * vc, axis=-1, keepdims=True)
    vb = (vc * lax.rsqrt(var + LN_EPS) * lng_ref[...] + lnb_ref[...]).astype(BF16)
    gd = width // A_GROUPS
    row = lax.broadcasted_iota(jnp.int32, (CHUNK, CHUNK), 0)
    col = lax.broadcasted_iota(jnp.int32, (CHUNK, CHUNK), 1)
    causal = row >= col
    for g in range(A_GROUPS):
        w = jnp.where(causal, ws_ref[g], 0.0).astype(BF16)
        rhs = jnp.concatenate(
            [vb[c * CHUNK:(c + 1) * CHUNK, g * gd:(g + 1) * gd] for c in range(n_chunks)], axis=1)
        s = jnp.dot(w, rhs, preferred_element_type=F32)
        bias = bs_ref[g]
        for c in range(n_chunks):
            sc = s[:, c * gd:(c + 1) * gd] + bias
            uc = u[c * CHUNK:(c + 1) * CHUNK, g * gd:(g + 1) * gd]
            us_scr[c * CHUNK:(c + 1) * CHUNK, g * gd:(g + 1) * gd] = (uc * sc).astype(BF16)
    o_ref[...] = x + jnp.dot(us_scr[...], wout_ref[...].astype(BF16), preferred_element_type=F32)


def _gmlp_mixer(x, gn, w_in, ln_g, ln_b, w_s, b_s, w_out):
    t, d = x.shape
    width = w_in.shape[1] // 2
    gd = width // A_GROUPS
    tm = min(TM_GMLP, t)
    n_chunks = tm // CHUNK
    bias = jnp.broadcast_to(b_s[:, :, None], (A_GROUPS, CHUNK, gd))
    const = lambda *shape: pl.BlockSpec(shape, lambda i: (0,) * len(shape))
    return pl.pallas_call(
        functools.partial(_gmlp_kernel, width=width, n_chunks=n_chunks),
        out_shape=jax.ShapeDtypeStruct((t, d), F32),
        grid=(t // tm,),
        in_specs=[
            pl.BlockSpec((tm, d), lambda i: (i, 0)),
            const(1, d), const(d, 2 * width), const(1, width), const(1, width),
            const(A_GROUPS, CHUNK, CHUNK), const(A_GROUPS, CHUNK, gd), const(width, d),
        ],
        out_specs=pl.BlockSpec((tm, d), lambda i: (i, 0)),
        scratch_shapes=[pltpu.VMEM((tm, width), BF16)],
        compiler_params=_cparams("parallel"),
        name="gmlp_mixer",
    )(x, gn.reshape(1, d), w_in, ln_g.reshape(1, width), ln_b.reshape(1, width), w_s, bias, w_out)


def _pool_kernel(x_ref, xh_ref, gn_ref, win_ref, wgrp_ref, scale_ref, wout_ref, o_ref, d_scr,
                 *, seq, tm):
    i = pl.program_id(0)
    x = x_ref[...]
    width = win_ref.shape[1]
    gd = width // len(POOL_WINDOWS)
    t0 = (i * tm) % seq
    xa = jnp.concatenate([xh_ref[...], x], axis=0)
    p_all = _bdot(_rms(xa, gn_ref[...]), win_ref[...])
    r = lax.broadcasted_iota(jnp.int32, (POOL_HALO + tm, 1), 0)
    p_all = jnp.where(r + (t0 - POOL_HALO) >= 0, p_all, 0.0)
    pos = lax.broadcasted_iota(jnp.int32, (tm, 1), 0) + (t0 + 1)

    sums = p_all
    have = 1
    for gi, win in enumerate(POOL_WINDOWS):
        while have < win:
            sums = sums[have:] + sums[:-have]
            have *= 2
        lo, hi = gi * gd, (gi + 1) * gd
        s = sums[POOL_HALO - (win - 1):POOL_HALO - (win - 1) + tm, lo:hi]
        count = jnp.minimum(pos, win).astype(F32)
        d_scr[:, lo:hi] = (s / count - p_all[POOL_HALO:, lo:hi]).astype(BF16)
    ys = []
    for gi in range(len(POOL_WINDOWS)):
        lo, hi = gi * gd, (gi + 1) * gd
        ys.append(jnp.dot(d_scr[:, lo:hi], wgrp_ref[gi].astype(BF16), preferred_element_type=F32))
    y = jnp.concatenate(ys, axis=1) * scale_ref[...]
    o_ref[...] = x + _bdot(y, wout_ref[...])


def _pool_mixer(x, gn, w_in, w_grp, scale, w_out, seq):
    t, d = x.shape
    width = w_in.shape[1]
    ng = len(POOL_WINDOWS)
    gd = width // ng
    tm = min(TM_POOL, seq)
    hb = tm // POOL_HALO
    const = lambda *shape: pl.BlockSpec(shape, lambda i: (0,) * len(shape))
    return pl.pallas_call(
        functools.partial(_pool_kernel, seq=seq, tm=tm),
        out_shape=jax.ShapeDtypeStruct((t, d), F32),
        grid=(t // tm,),
        in_specs=[
            pl.BlockSpec((tm, d), lambda i: (i, 0)),
            pl.BlockSpec((POOL_HALO, d), lambda i: (jnp.maximum(i * hb - 1, 0), 0)),
            const(1, d), const(d, width), const(ng, gd, gd), const(1, width), const(width, d),
        ],
        out_specs=pl.BlockSpec((tm, d), lambda i: (i, 0)),
        scratch_shapes=[pltpu.VMEM((tm, width), BF16)],
        compiler_params=_cparams("parallel"),
        name="pool_mixer",
    )(x, x, gn.reshape(1, d), w_in, w_grp, scale.reshape(1, width), w_out)


def _ffn_kernel(x_ref, gn_ref, wg_ref, wu_ref, wd_ref, o_ref, h_scr, acc_scr):
    j = pl.program_id(1)

    @pl.when(j == 0)
    def _():
        h_scr[...] = _rms(x_ref[...], gn_ref[...]).astype(BF16)
        acc_scr[...] = jnp.zeros_like(acc_scr)

    h = h_scr[...]
    gate = jnp.dot(h, wg_ref[...].astype(BF16), preferred_element_type=F32)
    up = jnp.dot(h, wu_ref[...].astype(BF16), preferred_element_type=F32)
    act = (gate * jax.nn.sigmoid(gate) * up).astype(BF16)
    acc_scr[...] += jnp.dot(act, wd_ref[...].astype(BF16), preferred_element_type=F32)

    @pl.when(j == pl.num_programs(1) - 1)
    def _():
        o_ref[...] = x_ref[...] + acc_scr[...]


def _ffn_dense(x, gn, w_gate, w_up, w_down, layer):
    t, d = x.shape
    f = w_gate.shape[2]
    tm = min(TM_FFN, t)
    tf = min(TF_FFN, f)
    return pl.pallas_call(
        _ffn_kernel,
        out_shape=jax.ShapeDtypeStruct((t, d), F32),
        grid=(t // tm, f // tf),
        in_specs=[
            pl.BlockSpec((tm, d), lambda i, j: (i, 0)),
            pl.BlockSpec((1, d), lambda i, j: (0, 0)),
            pl.BlockSpec((None, d, tf), lambda i, j: (layer, 0, j)),
            pl.BlockSpec((None, d, tf), lambda i, j: (layer, 0, j)),
            pl.BlockSpec((None, tf, d), lambda i, j: (layer, j, 0)),
        ],
        out_specs=pl.BlockSpec((tm, d), lambda i, j: (i, 0)),
        scratch_shapes=[pltpu.VMEM((tm, d), BF16), pltpu.VMEM((tm, d), F32)],
        compiler_params=_cparams("parallel", "arbitrary"),
        name="ffn_dense",
    )(x, gn.reshape(1, d), w_gate, w_up, w_down)


def _router_kernel(x_ref, gn_ref, wr_ref, idx_ref, gate_ref):
    h = _rms(x_ref[...], gn_ref[...])
    logits = jnp.dot(h, wr_ref[...], preferred_element_type=F32, precision=lax.Precision.HIGHEST)
    lane = lax.broadcasted_iota(jnp.int32, logits.shape, 1)
    lane_f = lane.astype(F32)
    neg = jnp.float32(-jnp.inf)
    logits = jnp.where(lane < N_EXPERTS, logits, neg)
    m1 = jnp.max(logits, axis=-1, keepdims=True)
    i1 = jnp.min(jnp.where(logits == m1, lane_f, float(LANES)), axis=-1, keepdims=True)
    rest = jnp.where(lane_f == i1, neg, logits)
    m2 = jnp.max(rest, axis=-1, keepdims=True)
    i2 = jnp.min(jnp.where(rest == m2, lane_f, float(LANES)), axis=-1, keepdims=True)
    e = jnp.exp(m2 - m1)
    den = 1.0 + e
    idx_ref[...] = jnp.where(lane == 0, i1, jnp.where(lane == 1, i2, 0.0)).astype(jnp.int32)
    gate_ref[...] = jnp.where(lane == 0, 1.0 / den, jnp.where(lane == 1, e / den, 0.0))


def _moe_router(x, gn, router):
    t, d = x.shape
    tm = min(TM_ROUTER, t)
    wr = jnp.zeros((d, LANES), F32).at[:, :N_EXPERTS].set(router)
    return pl.pallas_call(
        _router_kernel,
        out_shape=(jax.ShapeDtypeStruct((t, LANES), jnp.int32), jax.ShapeDtypeStruct((t, LANES), F32)),
        grid=(t // tm,),
        in_specs=[
            pl.BlockSpec((tm, d), lambda i: (i, 0)),
            pl.BlockSpec((1, d), lambda i: (0, 0)),
            pl.BlockSpec((d, LANES), lambda i: (0, 0)),
        ],
        out_specs=(pl.BlockSpec((tm, LANES), lambda i: (i, 0)), pl.BlockSpec((tm, LANES), lambda i: (i, 0))),
        compiler_params=_cparams("parallel"),
        name="moe_router",
    )(x, gn.reshape(1, d), wr)


def _tok_copy(src, src_tok, dst, dst_tok, sem):
    return pltpu.make_async_copy(
        src.at[pl.ds(pl.multiple_of(src_tok * SUBLANES, SUBLANES), SUBLANES)],
        dst.at[pl.ds(pl.multiple_of(dst_tok * SUBLANES, SUBLANES), SUBLANES)], sem)


def _to_tiles(ref, base, val):
    n, d = val.shape
    for c in range(d // LANES):
        ref[pl.ds(base + c, n, stride=SUBLANES), :] = val[:, c * LANES:(c + 1) * LANES]


def _from_tiles(ref, base, n, d):
    return jnp.concatenate(
        [ref[pl.ds(base + c, n, stride=SUBLANES), :] for c in range(d // LANES)], axis=1)


def _dispatch_kernel(dest_ref, nv_ref, x_ref, gn_ref, buf_ref, h_scr, sem, *, tm):
    n = x_ref.shape[0]

    @pl.when(pl.program_id(0) == 0)
    def _():
        h_scr[...] = jnp.zeros_like(h_scr)

        def fills(blk):
            return [pltpu.make_async_copy(
                h_scr, buf_ref.at[pl.ds((blk * tm + q * n) * SUBLANES, n * SUBLANES)], sem)
                for q in range(tm // n)]

        for blk in range(nv_ref.shape[0]):
            @pl.when(nv_ref[blk] < tm)
            def _():
                for cp in fills(blk):
                    cp.start()
                for cp in fills(blk):
                    cp.wait()

    _to_tiles(h_scr, 0, _rms(x_ref[...], gn_ref[...]))

    def copies(r):
        return [_tok_copy(h_scr, r, buf_ref, dest_ref[TOP_K * r + k], sem) for k in range(TOP_K)]

    def start(g, c):
        for u in range(DMA_UNROLL):
            for k, cp in enumerate(copies(g * DMA_UNROLL + u)):
                cp.start(priority=k % 2)
        return c

    lax.fori_loop(0, n // DMA_UNROLL, start, 0)

    def wait(g, c):
        for u in range(DMA_UNROLL):
            for cp in copies(g * DMA_UNROLL + u):
                cp.wait()
        return c

    lax.fori_loop(0, n // DMA_UNROLL, wait, 0)


def _moe_dispatch(x, gn, dest, n_valid, n_rows, tm):
    t, d = x.shape
    tb = min(TB_MOE, t)
    tile_rows = d // LANES
    assert tile_rows == SUBLANES and tm % tb == 0
    return pl.pallas_call(
        functools.partial(_dispatch_kernel, tm=tm),
        out_shape=jax.ShapeDtypeStruct((n_rows * tile_rows, LANES), F32),
        grid=(t // tb,),
        in_specs=[
            pl.BlockSpec((TOP_K * tb,), lambda i: (i,), memory_space=pltpu.SMEM),
            pl.BlockSpec(memory_space=pltpu.SMEM),
            pl.BlockSpec((tb, d), lambda i: (i, 0)),
            pl.BlockSpec((1, d), lambda i: (0, 0)),
        ],
        out_specs=pl.BlockSpec(memory_space=pl.ANY),
        scratch_shapes=[pltpu.VMEM((tb * tile_rows, LANES), F32), pltpu.SemaphoreType.DMA],
        compiler_params=_cparams("arbitrary"),
        name="moe_dispatch",
    )(dest, n_valid, x, gn.reshape(1, d))


def _moe_ffn_kernel(be_ref, nv_ref, xg_ref, wg_ref, wu_ref, wd_ref, o_ref, h_scr, acc_scr):
    i = pl.program_id(0)
    j = pl.program_id(1)
    tm, d = h_scr.shape
    nv = nv_ref[i]

    @pl.when(j == 0)
    def _():
        h_scr[...] = _from_tiles(xg_ref, 0, tm, d).astype(BF16)
        acc_scr[...] = jnp.zeros_like(acc_scr)

    for s in range(tm // MOE_SUB):
        rs = slice(s * MOE_SUB, (s + 1) * MOE_SUB)

        @pl.when(nv > s * MOE_SUB)
        def _():
            h = h_scr[rs, :]
            gate = jnp.dot(h, wg_ref[...].astype(BF16), preferred_element_type=F32)
            up = jnp.dot(h, wu_ref[...].astype(BF16), preferred_element_type=F32)
            act = (gate * jax.nn.sigmoid(gate) * up).astype(BF16)
            acc_scr[rs, :] += jnp.dot(act, wd_ref[...].astype(BF16), preferred_element_type=F32)

    @pl.when(j == pl.num_programs(1) - 1)
    def _():
        _to_tiles(o_ref, 0, acc_scr[...])


def _moe_ffn(buf, block_expert, n_valid, w_gate, w_up, w_down, layer, d):
    n_rows = buf.shape[0] // SUBLANES
    f = w_gate.shape[3]
    tm = TM_MOE
    tf = min(TF_FFN, f)
    nb, nf = n_rows // tm, f // tf

    def fj(i, j, nv):
        return jnp.where(nv[i] > 0, j, nf - 1)

    grid_spec = pltpu.PrefetchScalarGridSpec(
        num_scalar_prefetch=2,
        grid=(nb, nf),
        in_specs=[
            pl.BlockSpec((tm * SUBLANES, LANES), lambda i, j, be, nv: (i, 0)),
            pl.BlockSpec((None, None, d, tf), lambda i, j, be, nv: (layer, be[i], 0, fj(i, j, nv))),
            pl.BlockSpec((None, None, d, tf), lambda i, j, be, nv: (layer, be[i], 0, fj(i, j, nv))),
            pl.BlockSpec((None, None, tf, d), lambda i, j, be, nv: (layer, be[i], fj(i, j, nv), 0)),
        ],
        out_specs=pl.BlockSpec((tm * SUBLANES, LANES), lambda i, j, be, nv: (i, 0)),
        scratch_shapes=[pltpu.VMEM((tm, d), BF16), pltpu.VMEM((tm, d), F32)],
    )
    return pl.pallas_call(
        _moe_ffn_kernel,
        out_shape=jax.ShapeDtypeStruct(buf.shape, F32),
        grid_spec=grid_spec,
        compiler_params=_cparams("arbitrary", "arbitrary"),
        name="moe_ffn",
    )(block_expert, n_valid, buf, w_gate, w_up, w_down)


def _combine_kernel(dest_ref, x_ref, gate_ref, gn_ref, y_ref, o_ref, rows_scr, sem, *, final_norm):
    n, d = x_ref.shape

    def copies(r):
        return [_tok_copy(y_ref, dest_ref[TOP_K * r + k], rows_scr, k * n + r, sem) for k in range(TOP_K)]

    def start(g, c):
        for u in range(DMA_UNROLL):
            for k, cp in enumerate(copies(g * DMA_UNROLL + u)):
                cp.start(priority=k % 2)
        return c

    lax.fori_loop(0, n // DMA_UNROLL, start, 0)

    def wait(g, c):
        for u in range(DMA_UNROLL):
            for cp in copies(g * DMA_UNROLL + u):
                cp.wait()
        return c

    lax.fori_loop(0, n // DMA_UNROLL, wait, 0)
    gates = gate_ref[...]
    out = x_ref[...]
    for k in range(TOP_K):
        out = out + _from_tiles(rows_scr, k * n * SUBLANES, n, d) * gates[:, k:k + 1]
    if final_norm:
        out = _rms(out, gn_ref[...])
    o_ref[...] = out


def _moe_combine(x, y_grouped, dest, gates, gn_final, final_norm):
    t, d = x.shape
    tb = min(TB_MOE, t)
    return pl.pallas_call(
        functools.partial(_combine_kernel, final_norm=final_norm),
        out_shape=jax.ShapeDtypeStruct((t, d), F32),
        grid=(t // tb,),
        in_specs=[
            pl.BlockSpec((TOP_K * tb,), lambda i: (i,), memory_space=pltpu.SMEM),
            pl.BlockSpec((tb, d), lambda i: (i, 0)),
            pl.BlockSpec((tb, LANES), lambda i: (i, 0)),
            pl.BlockSpec((1, d), lambda i: (0, 0)),
            pl.BlockSpec(memory_space=pl.ANY),
        ],
        out_specs=pl.BlockSpec((tb, d), lambda i: (i, 0)),
        scratch_shapes=[pltpu.VMEM((TOP_K * tb * SUBLANES, LANES), F32), pltpu.SemaphoreType.DMA],
        compiler_params=_cparams("arbitrary"),
        name="moe_combine",
    )(dest, x, gates, gn_final.reshape(1, d), y_grouped)


def _moe_layer(x, gn, router, w_gate, w_up, w_down, layer, gn_final, final_norm):
    t, d = x.shape
    tm = TM_MOE
    idx, gates = _moe_router(x, gn, router)
    flat_e = idx[:, :TOP_K].reshape(-1)
    onehot = (flat_e[:, None] == jnp.arange(N_EXPERTS, dtype=jnp.int32)[None, :]).astype(jnp.int32)
    csum = jnp.cumsum(onehot, axis=0)
    rank = jnp.sum(csum * onehot, axis=1) - 1
    counts = csum[-1]
    padded = (counts + tm - 1) // tm * tm
    pad_end = jnp.cumsum(padded)
    pad_start = pad_end - padded
    dest = (jnp.sum(pad_start[None, :] * onehot, axis=1) + rank).astype(jnp.int32)
    nb = -(-(t * TOP_K) // tm) + N_EXPERTS
    block_start = jnp.arange(nb, dtype=jnp.int32) * tm
    block_expert = jnp.minimum(
        jnp.sum((block_start[:, None] >= pad_end[None, :]).astype(jnp.int32), axis=1), N_EXPERTS - 1)
    n_valid = jnp.clip(counts[block_expert] - (block_start - pad_start[block_expert]), 0, tm).astype(jnp.int32)
    buf = _moe_dispatch(x, gn, dest, n_valid, nb * tm, tm)
    y_grouped = _moe_ffn(buf, block_expert.astype(jnp.int32), n_valid, w_gate, w_up, w_down, layer, d)
    return _moe_combine(x, y_grouped, dest, gates, gn_final, final_norm)


def _head_sum(a, hsum_ref, hexp_ref):
    return _split_dot(_split_dot(a, hsum_ref[...]), hexp_ref[...])


def _rwkv_proj_kernel(x_ref, xh_ref, gn_ref, mu_ref, wrkv_ref, w0_ref, w1_ref, w2_ref,
                      a0_ref, a1_ref, a2_ref, g1_ref, g2_ref, kk_ref, ka_ref, hsum_ref, hexp_ref,
                      r_ref, dec_ref, k_ref, v_ref, na_ref, b_ref, g_ref, x_scr):
    i = pl.program_id(0)
    nb, tt, d = x_ref.shape
    n_blk = d // LANES
    for b in range(nb):
        for c in range(n_blk):
            x_scr[c, pl.ds(b, tt, stride=nb), :] = x_ref[b, :, c * LANES:(c + 1) * LANES]
    gn = gn_ref[...]
    h = _rms(jnp.concatenate([x_scr[c] for c in range(n_blk)], axis=1), gn)
    x_last = jnp.concatenate([xh_ref[b, SUBLANES - 1:SUBLANES, :] for b in range(nb)], axis=0)
    h_last = jnp.where(i == 0, 0.0, _rms(x_last, gn))
    h_prev = jnp.concatenate([h_last, h[:-nb]], axis=0)
    xx = h_prev - h
    mu = mu_ref[...]
    xs = [h + xx * mu[n:n + 1, :] for n in range(6)]
    r = _bdot(xs[0], wrkv_ref[0])
    k = _bdot(xs[1], wrkv_ref[1])
    v = _bdot(xs[2], wrkv_ref[2])
    wl = w0_ref[...] + _bdot(jnp.tanh(_bdot(xs[3], w1_ref[...])), w2_ref[...])
    w = -jax.nn.softplus(-wl) - 0.5
    a = jax.nn.sigmoid(a0_ref[...] + _bdot(_bdot(xs[4], a1_ref[...]), a2_ref[...]))
    g = _bdot(jax.nn.sigmoid(_bdot(xs[5], g1_ref[...])), g2_ref[...])
    kk = k * kk_ref[...]
    norm = jnp.sqrt(_head_sum(kk * kk, hsum_ref, hexp_ref))
    kk = kk / jnp.maximum(norm, 1e-12)
    r_ref[...] = r
    dec_ref[...] = jnp.exp(-jnp.exp(w))
    k_ref[...] = k * (1.0 + (a - 1.0) * ka_ref[...])
    v_ref[...] = v
    na_ref[...] = -kk
    b_ref[...] = kk * a
    g_ref[...] = g


def _half_swap(a0, a1):
    lo = lax.broadcasted_iota(jnp.int32, a0.shape, 1) < LANES // 2
    return (jnp.where(lo, a0, pltpu.roll(a1, LANES // 2, 1)),
            jnp.where(lo, pltpu.roll(a0, LANES // 2, 1), a1))


def _scan_load_pair(ref, row0, nb):
    p0 = ref[pl.ds(row0, nb), :]
    p1 = ref[pl.ds(row0 + nb, nb), :]
    n_blk = p0.shape[1] // LANES
    even, odd = [], []
    for c in range(n_blk):
        e, o = _half_swap(p0[:, c * LANES:(c + 1) * LANES], p1[:, c * LANES:(c + 1) * LANES])
        even.append(e)
        odd.append(o)
    return jnp.concatenate(even + odd, axis=0).T


def _scan_store_pair(ref, row0, nb, y_pair):
    yt = y_pair.T
    n_blk = ref.shape[1] // LANES
    for c in range(n_blk):
        y0, y1 = _half_swap(yt[c * nb:(c + 1) * nb], yt[(n_blk + c) * nb:(n_blk + c + 1) * nb])
        ref[pl.ds(row0, nb), c * LANES:(c + 1) * LANES] = y0
        ref[pl.ds(row0 + nb, nb), c * LANES:(c + 1) * LANES] = y1


def _scan_kernel(r_ref, w_ref, k_ref, v_ref, a_ref, b_ref, y_ref, s_scr, t_scr, *, nb):
    n = s_scr.shape[0]
    refs = (r_ref, w_ref, k_ref, v_ref, a_ref, b_ref)
    R, W, K, V, A, B = range(6)
    n_pairs = r_ref.shape[0] // (2 * nb)

    @pl.when(pl.program_id(0) == 0)
    def _():
        s_scr[...] = jnp.zeros_like(s_scr)

    def load_pair(p, slot, which):
        row0 = pl.multiple_of(p * 2 * nb, 2 * nb)
        for q in which:
            t_scr[slot, q] = _scan_load_pair(refs[q], row0, nb)

    load_pair(0, 0, range(6))

    def pair(p, slot):
        p_next = jnp.minimum(p + 1, n_pairs - 1)
        ys = []
        for t2 in range(2):
            base = t2 * n
            load_pair(p_next, 1 - slot, range(3 * t2, 3 * t2 + 3))
            v = t_scr[slot, V, base:base + n, :]
            sa = jnp.zeros_like(v)
            for j in range(n):
                sa = sa + s_scr[j] * t_scr[slot, A, base + j:base + j + 1, :]

            def row(q, j):
                return t_scr[slot, q, pl.ds(base + j, 1), :]

            def pass2(jc, y):
                for u in range(SCAN_J_UNROLL):
                    j = jc * SCAN_J_UNROLL + u
                    s_new = s_scr[j] * row(W, j) + sa * row(B, j) + v * row(K, j)
                    s_scr[j] = s_new
                    y = y + s_new * row(R, j)
                return y

            ys.append(lax.fori_loop(0, n // SCAN_J_UNROLL, pass2, jnp.zeros_like(v)))
        _scan_store_pair(y_ref, pl.multiple_of(p * 2 * nb, 2 * nb), nb, jnp.concatenate(ys, axis=0))

    def two_pairs(pp, c):
        pair(2 * pp, 0)
        pair(2 * pp + 1, 1)
        return c

    lax.fori_loop(0, n_pairs // 2, two_pairs, 0)


def _rwkv_out_kernel(x_ref, y_ref, r_ref, k_ref, v_ref, g_ref, rk_ref, lng_ref, lnb_ref,
                     wo_ref, hsum_ref, hexp_ref, o_ref, m_scr):
    nb, tt, d = x_ref.shape
    n_blk = d // LANES
    y = y_ref[...]
    inv_n = 1.0 / HEAD_DIM
    mean = _head_sum(y, hsum_ref, hexp_ref) * inv_n
    yc = y - mean
    var = _head_sum(yc * yc, hsum_ref, hexp_ref) * inv_n
    yn = yc * lax.rsqrt(var + GN_EPS) * lng_ref[...] + lnb_ref[...]
    bonus = _head_sum(r_ref[...] * k_ref[...] * rk_ref[...], hsum_ref, hexp_ref) * v_ref[...]
    out = (yn + bonus) * g_ref[...]
    m = _bdot(out, wo_ref[...])
    for c in range(n_blk):
        m_scr[c] = m[:, c * LANES:(c + 1) * LANES]
    for b in range(nb):
        for c in range(n_blk):
            cs = slice(c * LANES, (c + 1) * LANES)
            o_ref[b, :, cs] = x_ref[b, :, cs] + m_scr[c, pl.ds(b, tt, stride=nb), :]


def _rwkv_mixer(x, gn, mu, w_rkv, w0, w1, w2, a0, a1, a2, g1, g2, k_k, k_a, r_k, ln_g, ln_b, w_o,
                batch, seq):
    t, d = x.shape
    heads = d // HEAD_DIM
    assert batch == SUBLANES and batch * heads == LANES
    tt = min(TT_RWKV, seq)
    head_of = jnp.arange(d, dtype=jnp.int32) // HEAD_DIM
    hsum = (head_of[:, None] == jnp.arange(LANES, dtype=jnp.int32)[None, :]).astype(BF16)
    hexp = hsum.T
    const = lambda *shape: pl.BlockSpec(shape, lambda i: (0,) * len(shape))
    x3 = x.reshape(batch, seq, d)
    xblk = pl.BlockSpec((batch, tt, d), lambda i: (0, i, 0))
    tok = pl.BlockSpec((tt * batch, d), lambda i: (i, 0))
    vec = lambda a: a.reshape(1, d)
    r, dec, k, v, na, b, g = pl.pallas_call(
        _rwkv_proj_kernel,
        out_shape=tuple(jax.ShapeDtypeStruct((t, d), F32) for _ in range(7)),
        grid=(seq // tt,),
        in_specs=[
            xblk,
            pl.BlockSpec((batch, SUBLANES, d), lambda i: (0, jnp.maximum(i * (tt // SUBLANES) - 1, 0), 0)),
            const(1, d), const(6, d), const(3, d, d),
            const(1, d), const(d, w1.shape[1]), const(w2.shape[0], d),
            const(1, d), const(d, a1.shape[1]), const(a2.shape[0], d),
            const(d, g1.shape[1]), const(g2.shape[0], d),
            const(1, d), const(1, d), const(d, LANES), const(LANES, d),
        ],
        out_specs=tuple(tok for _ in range(7)),
        scratch_shapes=[pltpu.VMEM((d // LANES, tt * batch, LANES), F32)],
        compiler_params=_cparams("parallel"),
        name="rwkv_proj",
    )(x3, x3, vec(gn), mu, w_rkv, vec(w0), w1, w2, vec(a0), a1, a2, g1, g2, vec(k_k), vec(k_a), hsum, hexp)

    ts = min(TT_SCAN, seq)
    blk = pl.BlockSpec((ts * batch, d), lambda i: (i, 0))
    y = pl.pallas_call(
        functools.partial(_scan_kernel, nb=batch),
        out_shape=jax.ShapeDtypeStruct((t, d), F32),
        grid=(seq // ts,),
        in_specs=[blk] * 6,
        out_specs=blk,
        scratch_shapes=[pltpu.VMEM((HEAD_DIM, HEAD_DIM, LANES), F32),
                        pltpu.VMEM((2, 6, 2 * HEAD_DIM, LANES), F32)],
        compiler_params=_cparams("arbitrary"),
        name="rwkv_scan",
    )(r, dec, k, v, na, b)

    out = pl.pallas_call(
        _rwkv_out_kernel,
        out_shape=jax.ShapeDtypeStruct((batch, seq, d), F32),
        grid=(seq // tt,),
        in_specs=[xblk] + [tok] * 5 + [const(1, d), const(1, d), const(1, d), const(d, d),
                                       const(d, LANES), const(LANES, d)],
        out_specs=xblk,
        scratch_shapes=[pltpu.VMEM((d // LANES, tt * batch, LANES), F32)],
        compiler_params=_cparams("parallel"),
        name="rwkv_out",
    )(x3, y, r, k, v, g, r_k.reshape(1, d), vec(ln_g), vec(ln_b), w_o, hsum, hexp)
    return out.reshape(t, d)


def kernel(x, norm_mix, norm_ffn, norm_final, a_w_in, a_ln_g, a_ln_b, a_w_s, a_b_s, a_w_out, b_w_in, b_w_grp, b_scale, b_w_out, c_mu, c_w_rkv, c_w0, c_w1, c_w2, c_a0, c_a1, c_a2, c_g1, c_g2, c_k_k, c_k_a, c_r_k, c_ln_g, c_ln_b, c_w_o, f_w_gate, f_w_up, f_w_down, m_router, m_w_gate, m_w_up, m_w_down):
    batch, seq, d = x.shape
    depth = norm_mix.shape[0]
    bf = lambda w: w.astype(BF16)
    xt = x.reshape(batch * seq, d)
    f_w = (bf(f_w_gate), bf(f_w_up), bf(f_w_down))
    m_w = (m_w_gate, m_w_up, m_w_down)
    for layer in range(depth):
        kind, j = layer % 3, layer // 3
        if kind == 0:
            xt = _gmlp_mixer(xt, norm_mix[layer], bf(a_w_in[j]), a_ln_g[j], a_ln_b[j], a_w_s[j], a_b_s[j],
                             bf(a_w_out[j]))
        elif kind == 1:
            xt = _pool_mixer(xt, norm_mix[layer], bf(b_w_in[j]), bf(b_w_grp[j]), b_scale[j], bf(b_w_out[j]), seq)
        else:
            xt = _rwkv_mixer(xt, norm_mix[layer], c_mu[j], bf(c_w_rkv[j]), c_w0[j], bf(c_w1[j]), bf(c_w2[j]),
                             c_a0[j], bf(c_a1[j]), bf(c_a2[j]), bf(c_g1[j]), bf(c_g2[j]),
                             c_k_k[j], c_k_a[j], c_r_k[j], c_ln_g[j], c_ln_b[j], bf(c_w_o[j]), batch, seq)
        j = layer // 2
        last = layer == depth - 1
        if layer % 2 == 0:
            xt = _ffn_dense(xt, norm_ffn[layer], *f_w, j)
            if last:
                xt = _final_norm(xt, norm_final)
        else:
            xt = _moe_layer(xt, norm_ffn[layer], m_router[j], *m_w, j, norm_final, last)
    return xt.reshape(batch, seq, d)


def _final_norm_kernel(x_ref, gn_ref, o_ref):
    o_ref[...] = _rms(x_ref[...], gn_ref[...])


def _final_norm(x, gn):
    t, d = x.shape
    tm = min(TM_FFN, t)
    return pl.pallas_call(
        _final_norm_kernel,
        out_shape=jax.ShapeDtypeStruct((t, d), F32),
        grid=(t // tm,),
        in_specs=[pl.BlockSpec((tm, d), lambda i: (i, 0)), pl.BlockSpec((1, d), lambda i: (0, 0))],
        out_specs=pl.BlockSpec((tm, d), lambda i: (i, 0)),
        compiler_params=_cparams("parallel"),
        name="final_norm",
    )(x, gn.reshape(1, d))
```

```python
import functools

import jax
import jax.numpy as jnp
from jax import lax
from jax.experimental import pallas as pl
from jax.experimental.pallas import tpu as pltpu

F32 = jnp.float32
BF16 = jnp.bfloat16

RMS_EPS = 1e-6
LN_EPS = 1e-5
GN_EPS = 64e-5

CHUNK = 128
A_GROUPS = 8
POOL_WINDOWS = (2, 4, 8, 16)
POOL_HALO = 16
HEAD_DIM = 64
N_EXPERTS = 8
TOP_K = 2

LANES = 128
SUBLANES = 8
VMEM_LIMIT = 56 * 1024 * 1024

TM_GMLP = 512
TM_FFN = 1024
TF_FFN = 512
TM_POOL = 512
TM_ROUTER = 512
TB_MOE = 512
TM_MOE = 1024
MOE_SUB = 512
DMA_UNROLL = 8
TT_RWKV = 32
TT_SCAN = 64
SCAN_J_UNROLL = 8


def _cparams(*sem):
    return pltpu.CompilerParams(dimension_semantics=sem, vmem_limit_bytes=VMEM_LIMIT)


def _rms(x, g):
    return x * lax.rsqrt(jnp.mean(x * x, axis=-1, keepdims=True) + RMS_EPS) * g


def _bdot(a, b):
    return jnp.dot(a.astype(BF16), b.astype(BF16), preferred_element_type=F32)


def _split_dot(a, b):
    hi = a.astype(BF16)
    lo = (a - hi.astype(F32)).astype(BF16)
    return (jnp.dot(hi, b, preferred_element_type=F32)
            + jnp.dot(lo, b, preferred_element_type=F32))


def _gmlp_kernel(x_ref, gn_ref, win_ref, lng_ref, lnb_ref, ws_ref, bs_ref, wout_ref,
                 o_ref, us_scr, *, width, n_chunks):
    x = x_ref[...]
    h = _rms(x, gn_ref[...])
    z = _bdot(h, win_ref[...])
    z = 0.5 * z * (1.0 + lax.erf(z * (2.0 ** -0.5)))
    u = z[:, :width]
    v = z[:, width:]
    mu = jnp.mean(v, axis=-1, keepdims=True)
    vc = v - mu
    var = jnp.mean(vc * vc, axis=-1, keepdims=True)
    vb = (vc * lax.rsqrt(var + LN_EPS) * lng_ref[...] + lnb_ref[...]).astype(BF16)
    gd = width // A_GROUPS
    row = lax.broadcasted_iota(jnp.int32, (CHUNK, CHUNK), 0)
    col = lax.broadcasted_iota(jnp.int32, (CHUNK, CHUNK), 1)
    causal = row >= col
    for g in range(A_GROUPS):
        w = jnp.where(causal, ws_ref[g], 0.0).astype(BF16)
        rhs = jnp.concatenate(
            [vb[c * CHUNK:(c + 1) * CHUNK, g * gd:(g + 1) * gd] for c in range(n_chunks)], axis=1)
        s = jnp.dot(w, rhs, preferred_element_type=F32)
        bias = bs_ref[g]
        for c in range(n_chunks):
            sc = s[:, c * gd:(c + 1) * gd] + bias
            uc = u[c * CHUNK:(c + 1) * CHUNK, g * gd:(g + 1) * gd]
            us_scr[c * CHUNK:(c + 1) * CHUNK, g * gd:(g + 1) * gd] = (uc * sc).astype(BF16)
    o_ref[...] = x + jnp.dot(us_scr[...], wout_ref[...].astype(BF16), preferred_element_type=F32)


def _gmlp_mixer(x, gn, w_in, ln_g, ln_b, w_s, b_s, w_out):
    t, d = x.shape
    width = w_in.shape[1] // 2
    gd = width // A_GROUPS
    tm = min(TM_GMLP, t)
    n_chunks = tm // CHUNK
    bias = jnp.broadcast_to(b_s[:, :, None], (A_GROUPS, CHUNK, gd))
    const = lambda *shape: pl.BlockSpec(shape, lambda i: (0,) * len(shape))
    return pl.pallas_call(
        functools.partial(_gmlp_kernel, width=width, n_chunks=n_chunks),
        out_shape=jax.ShapeDtypeStruct((t, d), F32),
        grid=(t // tm,),
        in_specs=[
            pl.BlockSpec((tm, d), lambda i: (i, 0)),
            const(1, d), const(d, 2 * width), const(1, width), const(1, width),
            const(A_GROUPS, CHUNK, CHUNK), const(A_GROUPS, CHUNK, gd), const(width, d),
        ],
        out_specs=pl.BlockSpec((tm, d), lambda i: (i, 0)),
        scratch_shapes=[pltpu.VMEM((tm, width), BF16)],
        compiler_params=_cparams("parallel"),
        name="gmlp_mixer",
    )(x, gn.reshape(1, d), w_in, ln_g.reshape(1, width), ln_b.reshape(1, width), w_s, bias, w_out)


def _pool_kernel(x_ref, xh_ref, gn_ref, win_ref, wgrp_ref, scale_ref, wout_ref, o_ref, d_scr,
                 *, seq, tm):
    i = pl.program_id(0)
    x = x_ref[...]
    width = win_ref.shape[1]
    gd = width // len(POOL_WINDOWS)
    t0 = (i * tm) % seq
    xa = jnp.concatenate([xh_ref[...], x], axis=0)
    p_all = _bdot(_rms(xa, gn_ref[...]), win_ref[...])
    r = lax.broadcasted_iota(jnp.int32, (POOL_HALO + tm, 1), 0)
    p_all = jnp.where(r + (t0 - POOL_HALO) >= 0, p_all, 0.0)
    pos = lax.broadcasted_iota(jnp.int32, (tm, 1), 0) + (t0 + 1)

    sums = p_all
    have = 1
    for gi, win in enumerate(POOL_WINDOWS):
        while have < win:
            sums = sums[have:] + sums[:-have]
            have *= 2
        lo, hi = gi * gd, (gi + 1) * gd
        s = sums[POOL_HALO - (win - 1):POOL_HALO - (win - 1) + tm, lo:hi]
        count = jnp.minimum(pos, win).astype(F32)
        d_scr[:, lo:hi] = (s / count - p_all[POOL_HALO:, lo:hi]).astype(BF16)
    ys = []
    for gi in range(len(POOL_WINDOWS)):
        lo, hi = gi * gd, (gi + 1) * gd
        ys.append(jnp.dot(d_scr[:, lo:hi], wgrp_ref[gi].astype(BF16), preferred_element_type=F32))
    y = jnp.concatenate(ys, axis=1) * scale_ref[...]
    o_ref[...] = x + _bdot(y, wout_ref[...])


def _pool_mixer(x, gn, w_in, w_grp, scale, w_out, seq):
    t, d = x.shape
    width = w_in.shape[1]
    ng = len(POOL_WINDOWS)
    gd = width // ng
    tm = min(TM_POOL, seq)
    hb = tm // POOL_HALO
    const = lambda *shape: pl.BlockSpec(shape, lambda i: (0,) * len(shape))
    return pl.pallas_call(
        functools.partial(_pool_kernel, seq=seq, tm=tm),
        out_shape=jax.ShapeDtypeStruct((t, d), F32),
        grid=(t // tm,),
        in_specs=[
            pl.BlockSpec((tm, d), lambda i: (i, 0)),
            pl.BlockSpec((POOL_HALO, d), lambda i: (jnp.maximum(i * hb - 1, 0), 0)),
            const(1, d), const(d, width), const(ng, gd, gd), const(1, width), const(width, d),
        ],
        out_specs=pl.BlockSpec((tm, d), lambda i: (i, 0)),
        scratch_shapes=[pltpu.VMEM((tm, width), BF16)],
        compiler_params=_cparams("parallel"),
        name="pool_mixer",
    )(x, x, gn.reshape(1, d), w_in, w_grp, scale.reshape(1, width), w_out)


def _ffn_kernel(x_ref, gn_ref, wg_ref, wu_ref, wd_ref, o_ref, h_scr, acc_scr):
    j = pl.program_id(1)

    @pl.when(j == 0)
    def _():
        h_scr[...] = _rms(x_ref[...], gn_ref[...]).astype(BF16)
        acc_scr[...] = jnp.zeros_like(acc_scr)

    h = h_scr[...]
    gate = jnp.dot(h, wg_ref[...].astype(BF16), preferred_element_type=F32)
    up = jnp.dot(h, wu_ref[...].astype(BF16), preferred_element_type=F32)
    act = (gate * jax.nn.sigmoid(gate) * up).astype(BF16)
    acc_scr[...] += jnp.dot(act, wd_ref[...].astype(BF16), preferred_element_type=F32)

    @pl.when(j == pl.num_programs(1) - 1)
    def _():
        o_ref[...] = x_ref[...] + acc_scr[...]


def _ffn_dense(x, gn, w_gate, w_up, w_down, layer):
    t, d = x.shape
    f = w_gate.shape[2]
    tm = min(TM_FFN, t)
    tf = min(TF_FFN, f)
    return pl.pallas_call(
        _ffn_kernel,
        out_shape=jax.ShapeDtypeStruct((t, d), F32),
        grid=(t // tm, f // tf),
        in_specs=[
            pl.BlockSpec((tm, d), lambda i, j: (i, 0)),
            pl.BlockSpec((1, d), lambda i, j: (0, 0)),
            pl.BlockSpec((None, d, tf), lambda i, j: (layer, 0, j)),
            pl.BlockSpec((None, d, tf), lambda i, j: (layer, 0, j)),
            pl.BlockSpec((None, tf, d), lambda i, j: (layer, j, 0)),
        ],
        out_specs=pl.BlockSpec((tm, d), lambda i, j: (i, 0)),
        scratch_shapes=[pltpu.VMEM((tm, d), BF16), pltpu.VMEM((tm, d), F32)],
        compiler_params=_cparams("parallel", "arbitrary"),
        name="ffn_dense",
    )(x, gn.reshape(1, d), w_gate, w_up, w_down)


def _router_kernel(x_ref, gn_ref, wr_ref, idx_ref, gate_ref):
    h = _rms(x_ref[...], gn_ref[...])
    logits = jnp.dot(h, wr_ref[...], preferred_element_type=F32, precision=lax.Precision.HIGHEST)
    lane = lax.broadcasted_iota(jnp.int32, logits.shape, 1)
    lane_f = lane.astype(F32)
    neg = jnp.float32(-jnp.inf)
    logits = jnp.where(lane < N_EXPERTS, logits, neg)
    m1 = jnp.max(logits, axis=-1, keepdims=True)
    i1 = jnp.min(jnp.where(logits == m1, lane_f, float(LANES)), axis=-1, keepdims=True)
    rest = jnp.where(lane_f == i1, neg, logits)
    m2 = jnp.max(rest, axis=-1, keepdims=True)
    i2 = jnp.min(jnp.where(rest == m2, lane_f, float(LANES)), axis=-1, keepdims=True)
    e = jnp.exp(m2 - m1)
    den = 1.0 + e
    idx_ref[...] = jnp.where(lane == 0, i1, jnp.where(lane == 1, i2, 0.0)).astype(jnp.int32)
    gate_ref[...] = jnp.where(lane == 0, 1.0 / den, jnp.where(lane == 1, e / den, 0.0))


def _moe_router(x, gn, router):
    t, d = x.shape
    tm = min(TM_ROUTER, t)
    wr = jnp.zeros((d, LANES), F32).at[:, :N_EXPERTS].set(router)
    return pl.pallas_call(
        _router_kernel,
        out_shape=(jax.ShapeDtypeStruct((t, LANES), jnp.int32), jax.ShapeDtypeStruct((t, LANES), F32)),
        grid=(t // tm,),
        in_specs=[
            pl.BlockSpec((tm, d), lambda i: (i, 0)),
            pl.BlockSpec((1, d), lambda i: (0, 0)),
            pl.BlockSpec((d, LANES), lambda i: (0, 0)),
        ],
        out_specs=(pl.BlockSpec((tm, LANES), lambda i: (i, 0)), pl.BlockSpec((tm, LANES), lambda i: (i, 0))),
        compiler_params=_cparams("parallel"),
        name="moe_router",
    )(x, gn.reshape(1, d), wr)


def _tok_copy(src, src_tok, dst, dst_tok, sem):
    return pltpu.make_async_copy(
        src.at[pl.ds(pl.multiple_of(src_tok * SUBLANES, SUBLANES), SUBLANES)],
        dst.at[pl.ds(pl.multiple_of(dst_tok * SUBLANES, SUBLANES), SUBLANES)], sem)


def _to_tiles(ref, base, val):
    n, d = val.shape
    for c in range(d // LANES):
        ref[pl.ds(base + c, n, stride=SUBLANES), :] = val[:, c * LANES:(c + 1) * LANES]


def _from_tiles(ref, base, n, d):
    return jnp.concatenate(
        [ref[pl.ds(base + c, n, stride=SUBLANES), :] for c in range(d // LANES)], axis=1)


def _dispatch_kernel(dest_ref, nv_ref, x_ref, gn_ref, buf_ref, h_scr, sem, *, tm):
    n = x_ref.shape[0]

    @pl.when(pl.program_id(0) == 0)
    def _():
        h_scr[...] = jnp.zeros_like(h_scr)

        def fills(blk):
            return [pltpu.make_async_copy(
                h_scr, buf_ref.at[pl.ds((blk * tm + q * n) * SUBLANES, n * SUBLANES)], sem)
                for q in range(tm // n)]

        for blk in range(nv_ref.shape[0]):
            @pl.when(nv_ref[blk] < tm)
            def _():
                for cp in fills(blk):
                    cp.start()
                for cp in fills(blk):
                    cp.wait()

    _to_tiles(h_scr, 0, _rms(x_ref[...], gn_ref[...]))

    def copies(r):
        return [_tok_copy(h_scr, r, buf_ref, dest_ref[TOP_K * r + k], sem) for k in range(TOP_K)]

    def start(g, c):
        for u in range(DMA_UNROLL):
            for k, cp in enumerate(copies(g * DMA_UNROLL + u)):
                cp.start(priority=k % 2)
        return c

    lax.fori_loop(0, n // DMA_UNROLL, start, 0)

    def wait(g, c):
        for u in range(DMA_UNROLL):
            for cp in copies(g * DMA_UNROLL + u):
                cp.wait()
        return c

    lax.fori_loop(0, n // DMA_UNROLL, wait, 0)


def _moe_dispatch(x, gn, dest, n_valid, n_rows, tm):
    t, d = x.shape
    tb = min(TB_MOE, t)
    tile_rows = d // LANES
    assert tile_rows == SUBLANES and tm % tb == 0
    return pl.pallas_call(
        functools.partial(_dispatch_kernel, tm=tm),
        out_shape=jax.ShapeDtypeStruct((n_rows * tile_rows, LANES), F32),
        grid=(t // tb,),
        in_specs=[
            pl.BlockSpec((TOP_K * tb,), lambda i: (i,), memory_space=pltpu.SMEM),
            pl.BlockSpec(memory_space=pltpu.SMEM),
            pl.BlockSpec((tb, d), lambda i: (i, 0)),
            pl.BlockSpec((1, d), lambda i: (0, 0)),
        ],
        out_specs=pl.BlockSpec(memory_space=pl.ANY),
        scratch_shapes=[pltpu.VMEM((tb * tile_rows, LANES), F32), pltpu.SemaphoreType.DMA],
        compiler_params=_cparams("arbitrary"),
        name="moe_dispatch",
    )(dest, n_valid, x, gn.reshape(1, d))


def _moe_ffn_kernel(be_ref, nv_ref, xg_ref, wg_ref, wu_ref, wd_ref, o_ref, h_scr, acc_scr):
    i = pl.program_id(0)
    j = pl.program_id(1)
    tm, d = h_scr.shape
    nv = nv_ref[i]

    @pl.when(j == 0)
    def _():
        h_scr[...] = _from_tiles(xg_ref, 0, tm, d).astype(BF16)
        acc_scr[...] = jnp.zeros_like(acc_scr)

    def swiglu_rows(rs):
        h = h_scr[rs, :]
        gate = jnp.dot(h, wg_ref[...].astype(BF16), preferred_element_type=F32)
        up = jnp.dot(h, wu_ref[...].astype(BF16), preferred_element_type=F32)
        act = (gate * jax.nn.sigmoid(gate) * up).astype(BF16)
        acc_scr[rs, :] += jnp.dot(act, wd_ref[...].astype(BF16), preferred_element_type=F32)

    @pl.when(nv > tm - MOE_SUB)
    def _():
        swiglu_rows(slice(0, tm))

    @pl.when(jnp.logical_and(nv > 0, nv <= tm - MOE_SUB))
    def _():
        for s in range(tm // MOE_SUB - 1):
            @pl.when(nv > s * MOE_SUB)
            def _():
                swiglu_rows(slice(s * MOE_SUB, (s + 1) * MOE_SUB))

    @pl.when(j == pl.num_programs(1) - 1)
    def _():
        _to_tiles(o_ref, 0, acc_scr[...])


def _moe_ffn(buf, block_expert, n_valid, w_gate, w_up, w_down, layer, d):
    n_rows = buf.shape[0] // SUBLANES
    f = w_gate.shape[3]
    tm = TM_MOE
    tf = min(TF_FFN, f)
    nb, nf = n_rows // tm, f // tf

    def fj(i, j, nv):
        return jnp.where(nv[i] > 0, j, nf - 1)

    grid_spec = pltpu.PrefetchScalarGridSpec(
        num_scalar_prefetch=2,
        grid=(nb, nf),
        in_specs=[
            pl.BlockSpec((tm * SUBLANES, LANES), lambda i, j, be, nv: (i, 0)),
            pl.BlockSpec((None, None, d, tf), lambda i, j, be, nv: (layer, be[i], 0, fj(i, j, nv))),
            pl.BlockSpec((None, None, d, tf), lambda i, j, be, nv: (layer, be[i], 0, fj(i, j, nv))),
            pl.BlockSpec((None, None, tf, d), lambda i, j, be, nv: (layer, be[i], fj(i, j, nv), 0)),
        ],
        out_specs=pl.BlockSpec((tm * SUBLANES, LANES), lambda i, j, be, nv: (i, 0)),
        scratch_shapes=[pltpu.VMEM((tm, d), BF16), pltpu.VMEM((tm, d), F32)],
    )
    return pl.pallas_call(
        _moe_ffn_kernel,
        out_shape=jax.ShapeDtypeStruct(buf.shape, F32),
        grid_spec=grid_spec,
        compiler_params=_cparams("arbitrary", "arbitrary"),
        name="moe_ffn",
    )(block_expert, n_valid, buf, w_gate, w_up, w_down)


def _combine_kernel(dest_ref, dest_next_ref, x_ref, gate_ref, gn_ref, y_ref, o_ref, rows_scr, sem, *,
                    final_norm):
    n, d = x_ref.shape
    i = pl.program_id(0)
    slot = i % 2
    slot_toks = TOP_K * n

    def copies(dref, sl, r):
        return [_tok_copy(y_ref, dref[TOP_K * r + k], rows_scr, sl * slot_toks + k * n + r, sem.at[sl])
                for k in range(TOP_K)]

    def gather(dref, sl):
        def start(g, c):
            for u in range(DMA_UNROLL):
                for k, cp in enumerate(copies(dref, sl, g * DMA_UNROLL + u)):
                    cp.start(priority=k % 2)
            return c

        lax.fori_loop(0, n // DMA_UNROLL, start, 0)

    @pl.when(i == 0)
    def _():
        gather(dest_ref, slot)

    @pl.when(i + 1 < pl.num_programs(0))
    def _():
        gather(dest_next_ref, 1 - slot)

    def wait(g, c):
        for u in range(DMA_UNROLL):
            for cp in copies(dest_ref, slot, g * DMA_UNROLL + u):
                cp.wait()
        return c

    lax.fori_loop(0, n // DMA_UNROLL, wait, 0)
    gates = gate_ref[...]
    out = x_ref[...]
    for k in range(TOP_K):
        out = out + _from_tiles(rows_scr, (slot * slot_toks + k * n) * SUBLANES, n, d) * gates[:, k:k + 1]
    if final_norm:
        out = _rms(out, gn_ref[...])
    o_ref[...] = out


def _moe_combine(x, y_grouped, dest, gates, gn_final, final_norm):
    t, d = x.shape
    tb = min(TB_MOE, t)
    n_steps = t // tb
    return pl.pallas_call(
        functools.partial(_combine_kernel, final_norm=final_norm),
        out_shape=jax.ShapeDtypeStruct((t, d), F32),
        grid=(n_steps,),
        in_specs=[
            pl.BlockSpec((TOP_K * tb,), lambda i: (i,), memory_space=pltpu.SMEM),
            pl.BlockSpec((TOP_K * tb,), lambda i: (jnp.minimum(i + 1, n_steps - 1),), memory_space=pltpu.SMEM),
            pl.BlockSpec((tb, d), lambda i: (i, 0)),
            pl.BlockSpec((tb, LANES), lambda i: (i, 0)),
            pl.BlockSpec((1, d), lambda i: (0, 0)),
            pl.BlockSpec(memory_space=pl.ANY),
        ],
        out_specs=pl.BlockSpec((tb, d), lambda i: (i, 0)),
        scratch_shapes=[pltpu.VMEM((2 * TOP_K * tb * SUBLANES, LANES), F32), pltpu.SemaphoreType.DMA((2,))],
        compiler_params=_cparams("arbitrary"),
        name="moe_combine",
    )(dest, dest, x, gates, gn_final.reshape(1, d), y_grouped)


def _moe_layer(x, gn, router, w_gate, w_up, w_down, layer, gn_final, final_norm):
    t, d = x.shape
    tm = TM_MOE
    idx, gates = _moe_router(x, gn, router)
    flat_e = idx[:, :TOP_K].reshape(-1)
    onehot = (flat_e[:, None] == jnp.arange(N_EXPERTS, dtype=jnp.int32)[None, :]).astype(jnp.int32)
    csum = jnp.cumsum(onehot, axis=0)
    rank = jnp.sum(csum * onehot, axis=1) - 1
    counts = csum[-1]
    padded = (counts + tm - 1) // tm * tm
    pad_end = jnp.cumsum(padded)
    pad_start = pad_end - padded
    dest = (jnp.sum(pad_start[None, :] * onehot, axis=1) + rank).astype(jnp.int32)
    nb = -(-(t * TOP_K) // tm) + N_EXPERTS
    block_start = jnp.arange(nb, dtype=jnp.int32) * tm
    block_expert = jnp.minimum(
        jnp.sum((block_start[:, None] >= pad_end[None, :]).astype(jnp.int32), axis=1), N_EXPERTS - 1)
    n_valid = jnp.clip(counts[block_expert] - (block_start - pad_start[block_expert]), 0, tm).astype(jnp.int32)
    buf = _moe_dispatch(x, gn, dest, n_valid, nb * tm, tm)
    y_grouped = _moe_ffn(buf, block_expert.astype(jnp.int32), n_valid, w_gate, w_up, w_down, layer, d)
    return _moe_combine(x, y_grouped, dest, gates, gn_final, final_norm)


def _head_sum(a, hsum_ref, hexp_ref):
    return _split_dot(_split_dot(a, hsum_ref[...]), hexp_ref[...])


def _rwkv_proj_kernel(x_ref, xh_ref, gn_ref, mu_ref, wrkv_ref, w0_ref, w1_ref, w2_ref,
                      a0_ref, a1_ref, a2_ref, g1_ref, g2_ref, kk_ref, ka_ref, hsum_ref, hexp_ref,
                      r_ref, dec_ref, k_ref, v_ref, na_ref, b_ref, g_ref, x_scr):
    i = pl.program_id(0)
    nb, tt, d = x_ref.shape
    n_blk = d // LANES
    for b in range(nb):
        for c in range(n_blk):
            x_scr[c, pl.ds(b, tt, stride=nb), :] = x_ref[b, :, c * LANES:(c + 1) * LANES]
    gn = gn_ref[...]
    h = _rms(jnp.concatenate([x_scr[c] for c in range(n_blk)], axis=1), gn)
    x_last = jnp.concatenate([xh_ref[b, SUBLANES - 1:SUBLANES, :] for b in range(nb)], axis=0)
    h_last = jnp.where(i == 0, 0.0, _rms(x_last, gn))
    h_prev = jnp.concatenate([h_last, h[:-nb]], axis=0)
    xx = h_prev - h
    mu = mu_ref[...]
    xs = [h + xx * mu[n:n + 1, :] for n in range(6)]
    r = _bdot(xs[0], wrkv_ref[0])
    k = _bdot(xs[1], wrkv_ref[1])
    v = _bdot(xs[2], wrkv_ref[2])
    wl = w0_ref[...] + _bdot(jnp.tanh(_bdot(xs[3], w1_ref[...])), w2_ref[...])
    w = -jax.nn.softplus(-wl) - 0.5
    a = jax.nn.sigmoid(a0_ref[...] + _bdot(_bdot(xs[4], a1_ref[...]), a2_ref[...]))
    g = _bdot(jax.nn.sigmoid(_bdot(xs[5], g1_ref[...])), g2_ref[...])
    kk = k * kk_ref[...]
    norm = jnp.sqrt(_head_sum(kk * kk, hsum_ref, hexp_ref))
    kk = kk / jnp.maximum(norm, 1e-12)
    r_ref[...] = r
    dec_ref[...] = jnp.exp(-jnp.exp(w))
    k_ref[...] = k * (1.0 + (a - 1.0) * ka_ref[...])
    v_ref[...] = v
    na_ref[...] = -kk
    b_ref[...] = kk * a
    g_ref[...] = g


def _half_swap(a0, a1):
    lo = lax.broadcasted_iota(jnp.int32, a0.shape, 1) < LANES // 2
    return (jnp.where(lo, a0, pltpu.roll(a1, LANES // 2, 1)),
            jnp.where(lo, pltpu.roll(a0, LANES // 2, 1), a1))


def _scan_load_pair(ref, row0, nb):
    p0 = ref[pl.ds(row0, nb), :]
    p1 = ref[pl.ds(row0 + nb, nb), :]
    n_blk = p0.shape[1] // LANES
    even, odd = [], []
    for c in range(n_blk):
        e, o = _half_swap(p0[:, c * LANES:(c + 1) * LANES], p1[:, c * LANES:(c + 1) * LANES])
        even.append(e)
        odd.append(o)
    return jnp.concatenate(even + odd, axis=0).T


def _scan_store_pair(ref, row0, nb, y_pair):
    yt = y_pair.T
    n_blk = ref.shape[1] // LANES
    for c in range(n_blk):
        y0, y1 = _half_swap(yt[c * nb:(c + 1) * nb], yt[(n_blk + c) * nb:(n_blk + c + 1) * nb])
        ref[pl.ds(row0, nb), c * LANES:(c + 1) * LANES] = y0
        ref[pl.ds(row0 + nb, nb), c * LANES:(c + 1) * LANES] = y1


def _scan_kernel(r_ref, w_ref, k_ref, v_ref, a_ref, b_ref, y_ref, s_scr, t_scr, *, nb):
    n = s_scr.shape[0]
    refs = (r_ref, w_ref, k_ref, v_ref, a_ref, b_ref)
    R, W, K, V, A, B = range(6)
    n_pairs = r_ref.shape[0] // (2 * nb)

    @pl.when(pl.program_id(0) == 0)
    def _():
        s_scr[...] = jnp.zeros_like(s_scr)

    def load_pair(p, slot, which):
        row0 = pl.multiple_of(p * 2 * nb, 2 * nb)
        for q in which:
            t_scr[slot, q] = _scan_load_pair(refs[q], row0, nb)

    load_pair(0, 0, range(6))

    def pair(p, slot):
        p_next = jnp.minimum(p + 1, n_pairs - 1)
        ys = []
        for t2 in range(2):
            base = t2 * n
            load_pair(p_next, 1 - slot, range(3 * t2, 3 * t2 + 3))
            v = t_scr[slot, V, base:base + n, :]
            sa = jnp.zeros_like(v)
            for j in range(n):
                sa = sa + s_scr[j] * t_scr[slot, A, base + j:base + j + 1, :]

            def row(q, j):
                return t_scr[slot, q, pl.ds(base + j, 1), :]

            def pass2(jc, y):
                for u in range(SCAN_J_UNROLL):
                    j = jc * SCAN_J_UNROLL + u
                    s_new = s_scr[j] * row(W, j) + sa * row(B, j) + v * row(K, j)
                    s_scr[j] = s_new
                    y = y + s_new * row(R, j)
                return y

            ys.append(lax.fori_loop(0, n // SCAN_J_UNROLL, pass2, jnp.zeros_like(v)))
        _scan_store_pair(y_ref, pl.multiple_of(p * 2 * nb, 2 * nb), nb, jnp.concatenate(ys, axis=0))

    def two_pairs(pp, c):
        pair(2 * pp, 0)
        pair(2 * pp + 1, 1)
        return c

    lax.fori_loop(0, n_pairs // 2, two_pairs, 0)


def _rwkv_out_kernel(x_ref, y_ref, r_ref, k_ref, v_ref, g_ref, rk_ref, lng_ref, lnb_ref,
                     wo_ref, hsum_ref, hexp_ref, o_ref, m_scr):
    nb, tt, d = x_ref.shape
    n_blk = d // LANES
    y = y_ref[...]
    inv_n = 1.0 / HEAD_DIM
    mean = _head_sum(y, hsum_ref, hexp_ref) * inv_n
    yc = y - mean
    var = _head_sum(yc * yc, hsum_ref, hexp_ref) * inv_n
    yn = yc * lax.rsqrt(var + GN_EPS) * lng_ref[...] + lnb_ref[...]
    bonus = _head_sum(r_ref[...] * k_ref[...] * rk_ref[...], hsum_ref, hexp_ref) * v_ref[...]
    out = (yn + bonus) * g_ref[...]
    m = _bdot(out, wo_ref[...])
    for c in range(n_blk):
        m_scr[c] = m[:, c * LANES:(c + 1) * LANES]
    for b in range(nb):
        for c in range(n_blk):
            cs = slice(c * LANES, (c + 1) * LANES)
            o_ref[b, :, cs] = x_ref[b, :, cs] + m_scr[c, pl.ds(b, tt, stride=nb), :]


def _rwkv_mixer(x, gn, mu, w_rkv, w0, w1, w2, a0, a1, a2, g1, g2, k_k, k_a, r_k, ln_g, ln_b, w_o,
                batch, seq):
    t, d = x.shape
    heads = d // HEAD_DIM
    assert batch == SUBLANES and batch * heads == LANES
    tt = min(TT_RWKV, seq)
    head_of = jnp.arange(d, dtype=jnp.int32) // HEAD_DIM
    hsum = (head_of[:, None] == jnp.arange(LANES, dtype=jnp.int32)[None, :]).astype(BF16)
    hexp = hsum.T
    const = lambda *shape: pl.BlockSpec(shape, lambda i: (0,) * len(shape))
    x3 = x.reshape(batch, seq, d)
    xblk = pl.BlockSpec((batch, tt, d), lambda i: (0, i, 0))
    tok = pl.BlockSpec((tt * batch, d), lambda i: (i, 0))
    vec = lambda a: a.reshape(1, d)
    r, dec, k, v, na, b, g = pl.pallas_call(
        _rwkv_proj_kernel,
        out_shape=tuple(jax.ShapeDtypeStruct((t, d), F32) for _ in range(7)),
        grid=(seq // tt,),
        in_specs=[
            xblk,
            pl.BlockSpec((batch, SUBLANES, d), lambda i: (0, jnp.maximum(i * (tt // SUBLANES) - 1, 0), 0)),
            const(1, d), const(6, d), const(3, d, d),
            const(1, d), const(d, w1.shape[1]), const(w2.shape[0], d),
            const(1, d), const(d, a1.shape[1]), const(a2.shape[0], d),
            const(d, g1.shape[1]), const(g2.shape[0], d),
            const(1, d), const(1, d), const(d, LANES), const(LANES, d),
        ],
        out_specs=tuple(tok for _ in range(7)),
        scratch_shapes=[pltpu.VMEM((d // LANES, tt * batch, LANES), F32)],
        compiler_params=_cparams("parallel"),
        name="rwkv_proj",
    )(x3, x3, vec(gn), mu, w_rkv, vec(w0), w1, w2, vec(a0), a1, a2, g1, g2, vec(k_k), vec(k_a), hsum, hexp)

    ts = min(TT_SCAN, seq)
    blk = pl.BlockSpec((ts * batch, d), lambda i: (i, 0))
    y = pl.pallas_call(
        functools.partial(_scan_kernel, nb=batch),
        out_shape=jax.ShapeDtypeStruct((t, d), F32),
        grid=(seq // ts,),
        in_specs=[blk] * 6,
        out_specs=blk,
        scratch_shapes=[pltpu.VMEM((HEAD_DIM, HEAD_DIM, LANES), F32),
                        pltpu.VMEM((2, 6, 2 * HEAD_DIM, LANES), F32)],
        compiler_params=_cparams("arbitrary"),
        name="rwkv_scan",
    )(r, dec, k, v, na, b)

    out = pl.pallas_call(
        _rwkv_out_kernel,
        out_shape=jax.ShapeDtypeStruct((batch, seq, d), F32),
        grid=(seq // tt,),
        in_specs=[xblk] + [tok] * 5 + [const(1, d), const(1, d), const(1, d), const(d, d),
                                       const(d, LANES), const(LANES, d)],
        out_specs=xblk,
        scratch_shapes=[pltpu.VMEM((d // LANES, tt * batch, LANES), F32)],
        compiler_params=_cparams("parallel"),
        name="rwkv_out",
    )(x3, y, r, k, v, g, r_k.reshape(1, d), vec(ln_g), vec(ln_b), w_o, hsum, hexp)
    return out.reshape(t, d)


def kernel(x, norm_mix, norm_ffn, norm_final, a_w_in, a_ln_g, a_ln_b, a_w_s, a_b_s, a_w_out, b_w_in, b_w_grp, b_scale, b_w_out, c_mu, c_w_rkv, c_w0, c_w1, c_w2, c_a0, c_a1, c_a2, c_g1, c_g2, c_k_k, c_k_a, c_r_k, c_ln_g, c_ln_b, c_w_o, f_w_gate, f_w_up, f_w_down, m_router, m_w_gate, m_w_up, m_w_down):
    batch, seq, d = x.shape
    depth = norm_mix.shape[0]
    bf = lambda w: w.astype(BF16)
    xt = x.reshape(batch * seq, d)
    f_w = (bf(f_w_gate), bf(f_w_up), bf(f_w_down))
    m_w = (m_w_gate, m_w_up, m_w_down)
    for layer in range(depth):
        kind, j = layer % 3, layer // 3
        if kind == 0:
            xt = _gmlp_mixer(xt, norm_mix[layer], bf(a_w_in[j]), a_ln_g[j], a_ln_b[j], a_w_s[j], a_b_s[j],
                             bf(a_w_out[j]))
        elif kind == 1:
            xt = _pool_mixer(xt, norm_mix[layer], bf(b_w_in[j]), bf(b_w_grp[j]), b_scale[j], bf(b_w_out[j]), seq)
        else:
            xt = _rwkv_mixer(xt, norm_mix[layer], c_mu[j], bf(c_w_rkv[j]), c_w0[j], bf(c_w1[j]), bf(c_w2[j]),
                             c_a0[j], bf(c_a1[j]), bf(c_a2[j]), bf(c_g1[j]), bf(c_g2[j]),
                             c_k_k[j], c_k_a[j], c_r_k[j], c_ln_g[j], c_ln_b[j], bf(c_w_o[j]), batch, seq)
        j = layer // 2
        last = layer == depth - 1
        if layer % 2 == 0:
            xt = _ffn_dense(xt, norm_ffn[layer], *f_w, j)
            if last:
                xt = _final_norm(xt, norm_final)
        else:
            xt = _moe_layer(xt, norm_ffn[layer], m_router[j], *m_w, j, norm_final, last)
    return xt.reshape(batch, seq, d)


def _final_norm_kernel(x_ref, gn_ref, o_ref):
    o_ref[...] = _rms(x_ref[...], gn_ref[...])


def _final_norm(x, gn):
    t, d = x.shape
    tm = min(TM_FFN, t)
    return pl.pallas_call(
        _final_norm_kernel,
        out_shape=jax.ShapeDtypeStruct((t, d), F32),
        grid=(t // tm,),
        in_specs=[pl.BlockSpec((tm, d), lambda i: (i, 0)), pl.BlockSpec((1, d), lambda i: (0, 0))],
        out_specs=pl.BlockSpec((tm, d), lambda i: (i, 0)),
        compiler_params=_cparams("parallel"),
        name="final_norm",
    )(x, gn.reshape(1, d))
```

```python
import functools

import jax
import jax.numpy as jnp
from jax import lax
from jax.experimental import pallas as pl
from jax.experimental.pallas import tpu as pltpu

F32 = jnp.float32
BF16 = jnp.bfloat16

RMS_EPS = 1e-6
LN_EPS = 1e-5
GN_EPS = 64e-5

CHUNK = 128
A_GROUPS = 8
POOL_WINDOWS = (2, 4, 8, 16)
POOL_HALO = 16
HEAD_DIM = 64
N_EXPERTS = 8
TOP_K = 2

LANES = 128
SUBLANES = 8
VMEM_LIMIT = 56 * 1024 * 1024

TM_GMLP = 512
TM_FFN = 1024
TF_FFN = 512
TM_POOL = 512
TM_ROUTER = 512
TB_MOE = 512
TM_MOE = 1024
MOE_SUB = 512
DMA_UNROLL = 8
TT_RWKV = 32
TT_SCAN = 64
SCAN_J_UNROLL = 8


def _cparams(*sem):
    return pltpu.CompilerParams(dimension_semantics=sem, vmem_limit_bytes=VMEM_LIMIT)


def _rms(x, g):
    return x * lax.rsqrt(jnp.mean(x * x, axis=-1, keepdims=True) + RMS_EPS) * g


def _bdot(a, b):
    return jnp.dot(a.astype(BF16), b.astype(BF16), preferred_element_type=F32)


def _split_dot(a, b):
    hi = a.astype(BF16)
    lo = (a - hi.astype(F32)).astype(BF16)
    return (jnp.dot(hi, b, preferred_element_type=F32)
            + jnp.dot(lo, b, preferred_element_type=F32))


def _gmlp_kernel(x_ref, gn_ref, win_ref, lng_ref, lnb_ref, ws_ref, bs_ref, wout_ref,
                 o_ref, us_scr, *, width, n_chunks):
    x = x_ref[...]
    h = _rms(x, gn_ref[...])
    z = _bdot(h, win_ref[...])
    z = 0.5 * z * (1.0 + lax.erf(z * (2.0 ** -0.5)))
    u = z[:, :width]
    v = z[:, width:]
    mu = jnp.mean(v, axis=-1, keepdims=True)
    vc = v - mu
    var = jnp.mean(vc * vc, axis=-1, keepdims=True)
    vb = (vc * lax.rsqrt(var + LN_EPS) * lng_ref[...] + lnb_ref[...]).astype(BF16)
    gd = width // A_GROUPS
    row = lax.broadcasted_iota(jnp.int32, (CHUNK, CHUNK), 0)
    col = lax.broadcasted_iota(jnp.int32, (CHUNK, CHUNK), 1)
    causal = row >= col
    for g in range(A_GROUPS):
        w = jnp.where(causal, ws_ref[g], 0.0).astype(BF16)
        rhs = jnp.concatenate(
            [vb[c * CHUNK:(c + 1) * CHUNK, g * gd:(g + 1) * gd] for c in range(n_chunks)], axis=1)
        s = jnp.dot(w, rhs, preferred_element_type=F32)
        bias = bs_ref[g]
        for c in range(n_chunks):
            sc = s[:, c * gd:(c + 1) * gd] + bias
            uc = u[c * CHUNK:(c + 1) * CHUNK, g * gd:(g + 1) * gd]
            us_scr[c * CHUNK:(c + 1) * CHUNK, g * gd:(g + 1) * gd] = (uc * sc).astype(BF16)
    o_ref[...] = x + jnp.dot(us_scr[...], wout_ref[...].astype(BF16), preferred_element_type=F32)


def _gmlp_mixer(x, gn, w_in, ln_g, ln_b, w_s, b_s, w_out):
    t, d = x.shape
    width = w_in.shape[1] // 2
    gd = width // A_GROUPS
    tm = min(TM_GMLP, t)
    n_chunks = tm // CHUNK
    bias = jnp.broadcast_to(b_s[:, :, None], (A_GROUPS, CHUNK, gd))
    const = lambda *shape: pl.BlockSpec(shape, lambda i: (0,) * len(shape))
    return pl.pallas_call(
        functools.partial(_gmlp_kernel, width=width, n_chunks=n_chunks),
        out_shape=jax.ShapeDtypeStruct((t, d), F32),
        grid=(t // tm,),
        in_specs=[
            pl.BlockSpec((tm, d), lambda i: (i, 0)),
            const(1, d), const(d, 2 * width), const(1, width), const(1, width),
            const(A_GROUPS, CHUNK, CHUNK), const(A_GROUPS, CHUNK, gd), const(width, d),
        ],
        out_specs=pl.BlockSpec((tm, d), lambda i: (i, 0)),
        scratch_shapes=[pltpu.VMEM((tm, width), BF16)],
        compiler_params=_cparams("parallel"),
        name="gmlp_mixer",
    )(x, gn.reshape(1, d), w_in, ln_g.reshape(1, width), ln_b.reshape(1, width), w_s, bias, w_out)


def _pool_kernel(x_ref, xh_ref, gn_ref, win_ref, wgrp_ref, scale_ref, wout_ref, o_ref, d_scr,
                 *, seq, tm):
    i = pl.program_id(0)
    x = x_ref[...]
    width = win_ref.shape[1]
    gd = width // len(POOL_WINDOWS)
    t0 = (i * tm) % seq
    xa = jnp.concatenate([xh_ref[...], x], axis=0)
    p_all = _bdot(_rms(xa, gn_ref[...]), win_ref[...])
    r = lax.broadcasted_iota(jnp.int32, (POOL_HALO + tm, 1), 0)
    p_all = jnp.where(r + (t0 - POOL_HALO) >= 0, p_all, 0.0)
    pos = lax.broadcasted_iota(jnp.int32, (tm, 1), 0) + (t0 + 1)

    sums = p_all
    have = 1
    for gi, win in enumerate(POOL_WINDOWS):
        while have < win:
            sums = sums[have:] + sums[:-have]
            have *= 2
        lo, hi = gi * gd, (gi + 1) * gd
        s = sums[POOL_HALO - (win - 1):POOL_HALO - (win - 1) + tm, lo:hi]
        count = jnp.minimum(pos, win).astype(F32)
        d_scr[:, lo:hi] = (s / count - p_all[POOL_HALO:, lo:hi]).astype(BF16)
    ys = []
    for gi in range(len(POOL_WINDOWS)):
        lo, hi = gi * gd, (gi + 1) * gd
        ys.append(jnp.dot(d_scr[:, lo:hi], wgrp_ref[gi].astype(BF16), preferred_element_type=F32))
    y = jnp.concatenate(ys, axis=1) * scale_ref[...]
    o_ref[...] = x + _bdot(y, wout_ref[...])


def _pool_mixer(x, gn, w_in, w_grp, scale, w_out, seq):
    t, d = x.shape
    width = w_in.shape[1]
    ng = len(POOL_WINDOWS)
    gd = width // ng
    tm = min(TM_POOL, seq)
    hb = tm // POOL_HALO
    const = lambda *shape: pl.BlockSpec(shape, lambda i: (0,) * len(shape))
    return pl.pallas_call(
        functools.partial(_pool_kernel, seq=seq, tm=tm),
        out_shape=jax.ShapeDtypeStruct((t, d), F32),
        grid=(t // tm,),
        in_specs=[
            pl.BlockSpec((tm, d), lambda i: (i, 0)),
            pl.BlockSpec((POOL_HALO, d), lambda i: (jnp.maximum(i * hb - 1, 0), 0)),
            const(1, d), const(d, width), const(ng, gd, gd), const(1, width), const(width, d),
        ],
        out_specs=pl.BlockSpec((tm, d), lambda i: (i, 0)),
        scratch_shapes=[pltpu.VMEM((tm, width), BF16)],
        compiler_params=_cparams("parallel"),
        name="pool_mixer",
    )(x, x, gn.reshape(1, d), w_in, w_grp, scale.reshape(1, width), w_out)


def _ffn_kernel(x_ref, gn_ref, wg_ref, wu_ref, wd_ref, o_ref, h_scr, acc_scr):
    j = pl.program_id(1)

    @pl.when(j == 0)
    def _():
        h_scr[...] = _rms(x_ref[...], gn_ref[...]).astype(BF16)
        acc_scr[...] = jnp.zeros_like(acc_scr)

    h = h_scr[...]
    gate = jnp.dot(h, wg_ref[...].astype(BF16), preferred_element_type=F32)
    up = jnp.dot(h, wu_ref[...].astype(BF16), preferred_element_type=F32)
    act = (gate * jax.nn.sigmoid(gate) * up).astype(BF16)
    acc_scr[...] += jnp.dot(act, wd_ref[...].astype(BF16), preferred_element_type=F32)

    @pl.when(j == pl.num_programs(1) - 1)
    def _():
        o_ref[...] = x_ref[...] + acc_scr[...]


def _ffn_dense(x, gn, w_gate, w_up, w_down, layer):
    t, d = x.shape
    f = w_gate.shape[2]
    tm = min(TM_FFN, t)
    tf = min(TF_FFN, f)
    return pl.pallas_call(
        _ffn_kernel,
        out_shape=jax.ShapeDtypeStruct((t, d), F32),
        grid=(t // tm, f // tf),
        in_specs=[
            pl.BlockSpec((tm, d), lambda i, j: (i, 0)),
            pl.BlockSpec((1, d), lambda i, j: (0, 0)),
            pl.BlockSpec((None, d, tf), lambda i, j: (layer, 0, j)),
            pl.BlockSpec((None, d, tf), lambda i, j: (layer, 0, j)),
            pl.BlockSpec((None, tf, d), lambda i, j: (layer, j, 0)),
        ],
        out_specs=pl.BlockSpec((tm, d), lambda i, j: (i, 0)),
        scratch_shapes=[pltpu.VMEM((tm, d), BF16), pltpu.VMEM((tm, d), F32)],
        compiler_params=_cparams("parallel", "arbitrary"),
        name="ffn_dense",
    )(x, gn.reshape(1, d), w_gate, w_up, w_down)


def _router_kernel(x_ref, gn_ref, wr_ref, idx_ref, gate_ref):
    h = _rms(x_ref[...], gn_ref[...])
    wr = wr_ref[...]
    w_hi = wr.astype(BF16)
    w_lo = (wr - w_hi.astype(F32)).astype(BF16)
    logits = _split_dot(h, w_hi) + jnp.dot(h.astype(BF16), w_lo, preferred_element_type=F32)
    lane = lax.broadcasted_iota(jnp.int32, logits.shape, 1)
    lane_f = lane.astype(F32)
    neg = jnp.float32(-jnp.inf)
    logits = jnp.where(lane < N_EXPERTS, logits, neg)
    m1 = jnp.max(logits, axis=-1, keepdims=True)
    i1 = jnp.min(jnp.where(logits == m1, lane_f, float(LANES)), axis=-1, keepdims=True)
    rest = jnp.where(lane_f == i1, neg, logits)
    m2 = jnp.max(rest, axis=-1, keepdims=True)
    i2 = jnp.min(jnp.where(rest == m2, lane_f, float(LANES)), axis=-1, keepdims=True)
    e = jnp.exp(m2 - m1)
    den = 1.0 + e
    idx_ref[...] = jnp.where(lane == 0, i1, jnp.where(lane == 1, i2, 0.0)).astype(jnp.int32)
    gate_ref[...] = jnp.where(lane == 0, 1.0 / den, jnp.where(lane == 1, e / den, 0.0))


def _moe_router(x, gn, router):
    t, d = x.shape
    tm = min(TM_ROUTER, t)
    wr = jnp.zeros((d, LANES), F32).at[:, :N_EXPERTS].set(router)
    return pl.pallas_call(
        _router_kernel,
        out_shape=(jax.ShapeDtypeStruct((t, LANES), jnp.int32), jax.ShapeDtypeStruct((t, LANES), F32)),
        grid=(t // tm,),
        in_specs=[
            pl.BlockSpec((tm, d), lambda i: (i, 0)),
            pl.BlockSpec((1, d), lambda i: (0, 0)),
            pl.BlockSpec((d, LANES), lambda i: (0, 0)),
        ],
        out_specs=(pl.BlockSpec((tm, LANES), lambda i: (i, 0)), pl.BlockSpec((tm, LANES), lambda i: (i, 0))),
        compiler_params=_cparams("parallel"),
        name="moe_router",
    )(x, gn.reshape(1, d), wr)


def _tok_copy(src, src_tok, dst, dst_tok, sem):
    return pltpu.make_async_copy(
        src.at[pl.ds(pl.multiple_of(src_tok * SUBLANES, SUBLANES), SUBLANES)],
        dst.at[pl.ds(pl.multiple_of(dst_tok * SUBLANES, SUBLANES), SUBLANES)], sem)


def _to_tiles(ref, base, val):
    n, d = val.shape
    for c in range(d // LANES):
        ref[pl.ds(base + c, n, stride=SUBLANES), :] = val[:, c * LANES:(c + 1) * LANES]


def _from_tiles(ref, base, n, d):
    return jnp.concatenate(
        [ref[pl.ds(base + c, n, stride=SUBLANES), :] for c in range(d // LANES)], axis=1)


def _dispatch_kernel(dest_ref, nv_ref, x_ref, gn_ref, buf_ref, h_scr, sem, *, tm):
    n = x_ref.shape[0]

    @pl.when(pl.program_id(0) == 0)
    def _():
        h_scr[...] = jnp.zeros_like(h_scr)

        def fills(blk):
            return [pltpu.make_async_copy(
                h_scr, buf_ref.at[pl.ds((blk * tm + q * n) * SUBLANES, n * SUBLANES)], sem)
                for q in range(tm // n)]

        for blk in range(nv_ref.shape[0]):
            @pl.when(nv_ref[blk] < tm)
            def _():
                for q, cp in enumerate(fills(blk)):
                    cp.start(priority=q % 2)

        for blk in range(nv_ref.shape[0]):
            @pl.when(nv_ref[blk] < tm)
            def _():
                for cp in fills(blk):
                    cp.wait()

    _to_tiles(h_scr, 0, _rms(x_ref[...], gn_ref[...]))

    def copies(r):
        return [_tok_copy(h_scr, r, buf_ref, dest_ref[TOP_K * r + k], sem) for k in range(TOP_K)]

    def start(g, c):
        for u in range(DMA_UNROLL):
            for k, cp in enumerate(copies(g * DMA_UNROLL + u)):
                cp.start(priority=k % 2)
        return c

    lax.fori_loop(0, n // DMA_UNROLL, start, 0)

    def wait(g, c):
        for u in range(DMA_UNROLL):
            for cp in copies(g * DMA_UNROLL + u):
                cp.wait()
        return c

    lax.fori_loop(0, n // DMA_UNROLL, wait, 0)


def _moe_dispatch(x, gn, dest, n_valid, n_rows, tm):
    t, d = x.shape
    tb = min(TB_MOE, t)
    tile_rows = d // LANES
    assert tile_rows == SUBLANES and tm % tb == 0
    return pl.pallas_call(
        functools.partial(_dispatch_kernel, tm=tm),
        out_shape=jax.ShapeDtypeStruct((n_rows * tile_rows, LANES), F32),
        grid=(t // tb,),
        in_specs=[
            pl.BlockSpec((TOP_K * tb,), lambda i: (i,), memory_space=pltpu.SMEM),
            pl.BlockSpec(memory_space=pltpu.SMEM),
            pl.BlockSpec((tb, d), lambda i: (i, 0)),
            pl.BlockSpec((1, d), lambda i: (0, 0)),
        ],
        out_specs=pl.BlockSpec(memory_space=pl.ANY),
        scratch_shapes=[pltpu.VMEM((tb * tile_rows, LANES), F32), pltpu.SemaphoreType.DMA],
        compiler_params=_cparams("arbitrary"),
        name="moe_dispatch",
    )(dest, n_valid, x, gn.reshape(1, d))


def _moe_ffn_kernel(be_ref, nv_ref, xg_ref, wg_ref, wu_ref, wd_ref, o_ref, h_scr, acc_scr):
    i = pl.program_id(0)
    j = pl.program_id(1)
    tm, d = h_scr.shape
    nv = nv_ref[i]

    @pl.when(j == 0)
    def _():
        h_scr[...] = _from_tiles(xg_ref, 0, tm, d).astype(BF16)
        acc_scr[...] = jnp.zeros_like(acc_scr)

    def swiglu_rows(rs):
        h = h_scr[rs, :]
        gate = jnp.dot(h, wg_ref[...].astype(BF16), preferred_element_type=F32)
        up = jnp.dot(h, wu_ref[...].astype(BF16), preferred_element_type=F32)
        act = (gate * jax.nn.sigmoid(gate) * up).astype(BF16)
        acc_scr[rs, :] += jnp.dot(act, wd_ref[...].astype(BF16), preferred_element_type=F32)

    @pl.when(nv > tm - MOE_SUB)
    def _():
        swiglu_rows(slice(0, tm))

    @pl.when(jnp.logical_and(nv > 0, nv <= tm - MOE_SUB))
    def _():
        for s in range(tm // MOE_SUB - 1):
            @pl.when(nv > s * MOE_SUB)
            def _():
                swiglu_rows(slice(s * MOE_SUB, (s + 1) * MOE_SUB))

    @pl.when(j == pl.num_programs(1) - 1)
    def _():
        _to_tiles(o_ref, 0, acc_scr[...])


def _moe_ffn(buf, block_expert, n_valid, w_gate, w_up, w_down, layer, d):
    n_rows = buf.shape[0] // SUBLANES
    f = w_gate.shape[3]
    tm = TM_MOE
    tf = min(TF_FFN, f)
    nb, nf = n_rows // tm, f // tf

    def fj(i, j, nv):
        return jnp.where(nv[i] > 0, j, nf - 1)

    grid_spec = pltpu.PrefetchScalarGridSpec(
        num_scalar_prefetch=2,
        grid=(nb, nf),
        in_specs=[
            pl.BlockSpec((tm * SUBLANES, LANES), lambda i, j, be, nv: (i, 0)),
            pl.BlockSpec((None, None, d, tf), lambda i, j, be, nv: (layer, be[i], 0, fj(i, j, nv))),
            pl.BlockSpec((None, None, d, tf), lambda i, j, be, nv: (layer, be[i], 0, fj(i, j, nv))),
            pl.BlockSpec((None, None, tf, d), lambda i, j, be, nv: (layer, be[i], fj(i, j, nv), 0)),
        ],
        out_specs=pl.BlockSpec((tm * SUBLANES, LANES), lambda i, j, be, nv: (i, 0)),
        scratch_shapes=[pltpu.VMEM((tm, d), BF16), pltpu.VMEM((tm, d), F32)],
    )
    return pl.pallas_call(
        _moe_ffn_kernel,
        out_shape=jax.ShapeDtypeStruct(buf.shape, F32),
        grid_spec=grid_spec,
        compiler_params=_cparams("arbitrary", "arbitrary"),
        name="moe_ffn",
    )(block_expert, n_valid, buf, w_gate, w_up, w_down)


def _combine_kernel(dest_ref, dest_next_ref, x_ref, gate_ref, gn_ref, y_ref, o_ref, rows_scr, sem, *,
                    final_norm):
    n, d = x_ref.shape
    i = pl.program_id(0)
    slot = i % 2
    slot_toks = TOP_K * n

    def copies(dref, sl, r):
        return [_tok_copy(y_ref, dref[TOP_K * r + k], rows_scr, sl * slot_toks + k * n + r, sem.at[sl])
                for k in range(TOP_K)]

    def gather(dref, sl):
        def start(g, c):
            for u in range(DMA_UNROLL):
                for k, cp in enumerate(copies(dref, sl, g * DMA_UNROLL + u)):
                    cp.start(priority=k % 2)
            return c

        lax.fori_loop(0, n // DMA_UNROLL, start, 0)

    @pl.when(i == 0)
    def _():
        gather(dest_ref, slot)

    @pl.when(i + 1 < pl.num_programs(0))
    def _():
        gather(dest_next_ref, 1 - slot)

    def wait(g, c):
        for u in range(DMA_UNROLL):
            for cp in copies(dest_ref, slot, g * DMA_UNROLL + u):
                cp.wait()
        return c

    lax.fori_loop(0, n // DMA_UNROLL, wait, 0)
    gates = gate_ref[...]
    out = x_ref[...]
    for k in range(TOP_K):
        out = out + _from_tiles(rows_scr, (slot * slot_toks + k * n) * SUBLANES, n, d) * gates[:, k:k + 1]
    if final_norm:
        out = _rms(out, gn_ref[...])
    o_ref[...] = out


def _moe_combine(x, y_grouped, dest, gates, gn_final, final_norm):
    t, d = x.shape
    tb = min(TB_MOE, t)
    n_steps = t // tb
    return pl.pallas_call(
        functools.partial(_combine_kernel, final_norm=final_norm),
        out_shape=jax.ShapeDtypeStruct((t, d), F32),
        grid=(n_steps,),
        in_specs=[
            pl.BlockSpec((TOP_K * tb,), lambda i: (i,), memory_space=pltpu.SMEM),
            pl.BlockSpec((TOP_K * tb,), lambda i: (jnp.minimum(i + 1, n_steps - 1),), memory_space=pltpu.SMEM),
            pl.BlockSpec((tb, d), lambda i: (i, 0)),
            pl.BlockSpec((tb, LANES), lambda i: (i, 0)),
            pl.BlockSpec((1, d), lambda i: (0, 0)),
            pl.BlockSpec(memory_space=pl.ANY),
        ],
        out_specs=pl.BlockSpec((tb, d), lambda i: (i, 0)),
        scratch_shapes=[pltpu.VMEM((2 * TOP_K * tb * SUBLANES, LANES), F32), pltpu.SemaphoreType.DMA((2,))],
        compiler_params=_cparams("arbitrary"),
        name="moe_combine",
    )(dest, dest, x, gates, gn_final.reshape(1, d), y_grouped)


def _moe_layer(x, gn, router, w_gate, w_up, w_down, layer, gn_final, final_norm):
    t, d = x.shape
    tm = TM_MOE
    idx, gates = _moe_router(x, gn, router)
    flat_e = idx[:, :TOP_K].reshape(-1)
    onehot = (flat_e[:, None] == jnp.arange(N_EXPERTS, dtype=jnp.int32)[None, :]).astype(jnp.int32)
    csum = jnp.cumsum(onehot, axis=0)
    rank = jnp.sum(csum * onehot, axis=1) - 1
    counts = csum[-1]
    padded = (counts + tm - 1) // tm * tm
    pad_end = jnp.cumsum(padded)
    pad_start = pad_end - padded
    dest = (jnp.sum(pad_start[None, :] * onehot, axis=1) + rank).astype(jnp.int32)
    nb = -(-(t * TOP_K) // tm) + N_EXPERTS
    block_start = jnp.arange(nb, dtype=jnp.int32) * tm
    block_expert = jnp.minimum(
        jnp.sum((block_start[:, None] >= pad_end[None, :]).astype(jnp.int32), axis=1), N_EXPERTS - 1)
    n_valid = jnp.clip(counts[block_expert] - (block_start - pad_start[block_expert]), 0, tm).astype(jnp.int32)
    buf = _moe_dispatch(x, gn, dest, n_valid, nb * tm, tm)
    y_grouped = _moe_ffn(buf, block_expert.astype(jnp.int32), n_valid, w_gate, w_up, w_down, layer, d)
    return _moe_combine(x, y_grouped, dest, gates, gn_final, final_norm)


def _head_sum(a, hsum_ref, hexp_ref):
    return _split_dot(_split_dot(a, hsum_ref[...]), hexp_ref[...])


def _rwkv_proj_kernel(x_ref, xh_ref, gn_ref, mu_ref, wrkv_ref, w0_ref, w1_ref, w2_ref,
                      a0_ref, a1_ref, a2_ref, g1_ref, g2_ref, kk_ref, ka_ref, hsum_ref, hexp_ref,
                      r_ref, dec_ref, k_ref, v_ref, na_ref, b_ref, g_ref, x_scr):
    i = pl.program_id(0)
    nb, tt, d = x_ref.shape
    n_blk = d // LANES
    for b in range(nb):
        for c in range(n_blk):
            x_scr[c, pl.ds(b, tt, stride=nb), :] = x_ref[b, :, c * LANES:(c + 1) * LANES]
    gn = gn_ref[...]
    h = _rms(jnp.concatenate([x_scr[c] for c in range(n_blk)], axis=1), gn)
    x_last = jnp.concatenate([xh_ref[b, SUBLANES - 1:SUBLANES, :] for b in range(nb)], axis=0)
    h_last = jnp.where(i == 0, 0.0, _rms(x_last, gn))
    h_prev = jnp.concatenate([h_last, h[:-nb]], axis=0)
    xx = h_prev - h
    mu = mu_ref[...]
    xs = [h + xx * mu[n:n + 1, :] for n in range(6)]
    r = _bdot(xs[0], wrkv_ref[0])
    k = _bdot(xs[1], wrkv_ref[1])
    v = _bdot(xs[2], wrkv_ref[2])
    wl = w0_ref[...] + _bdot(jnp.tanh(_bdot(xs[3], w1_ref[...])), w2_ref[...])
    w = -jax.nn.softplus(-wl) - 0.5
    a = jax.nn.sigmoid(a0_ref[...] + _bdot(_bdot(xs[4], a1_ref[...]), a2_ref[...]))
    g = _bdot(jax.nn.sigmoid(_bdot(xs[5], g1_ref[...])), g2_ref[...])
    kk = k * kk_ref[...]
    norm = jnp.sqrt(_head_sum(kk * kk, hsum_ref, hexp_ref))
    kk = kk / jnp.maximum(norm, 1e-12)
    r_ref[...] = r
    dec_ref[...] = jnp.exp(-jnp.exp(w))
    k_ref[...] = k * (1.0 + (a - 1.0) * ka_ref[...])
    v_ref[...] = v
    na_ref[...] = -kk
    b_ref[...] = kk * a
    g_ref[...] = g


def _half_swap(a0, a1):
    lo = lax.broadcasted_iota(jnp.int32, a0.shape, 1) < LANES // 2
    return (jnp.where(lo, a0, pltpu.roll(a1, LANES // 2, 1)),
            jnp.where(lo, pltpu.roll(a0, LANES // 2, 1), a1))


def _scan_load_pair(ref, row0, nb):
    p0 = ref[pl.ds(row0, nb), :]
    p1 = ref[pl.ds(row0 + nb, nb), :]
    n_blk = p0.shape[1] // LANES
    even, odd = [], []
    for c in range(n_blk):
        e, o = _half_swap(p0[:, c * LANES:(c + 1) * LANES], p1[:, c * LANES:(c + 1) * LANES])
        even.append(e)
        odd.append(o)
    return jnp.concatenate(even + odd, axis=0).T


def _scan_store_pair(ref, row0, nb, y_pair):
    yt = y_pair.T
    n_blk = ref.shape[1] // LANES
    for c in range(n_blk):
        y0, y1 = _half_swap(yt[c * nb:(c + 1) * nb], yt[(n_blk + c) * nb:(n_blk + c + 1) * nb])
        ref[pl.ds(row0, nb), c * LANES:(c + 1) * LANES] = y0
        ref[pl.ds(row0 + nb, nb), c * LANES:(c + 1) * LANES] = y1


def _scan_kernel(r_ref, w_ref, k_ref, v_ref, a_ref, b_ref, y_ref, s_scr, t_scr, *, nb):
    n = s_scr.shape[0]
    refs = (r_ref, w_ref, k_ref, v_ref, a_ref, b_ref)
    R, W, K, V, A, B = range(6)
    n_pairs = r_ref.shape[0] // (2 * nb)

    @pl.when(pl.program_id(0) == 0)
    def _():
        s_scr[...] = jnp.zeros_like(s_scr)

    def load_pair(p, slot, which):
        row0 = pl.multiple_of(p * 2 * nb, 2 * nb)
        for q in which:
            t_scr[slot, q] = _scan_load_pair(refs[q], row0, nb)

    load_pair(0, 0, range(6))

    def pair(p, slot):
        p_next = jnp.minimum(p + 1, n_pairs - 1)
        ys = []
        for t2 in range(2):
            base = t2 * n
            load_pair(p_next, 1 - slot, range(3 * t2, 3 * t2 + 3))
            v = t_scr[slot, V, base:base + n, :]
            sa = jnp.zeros_like(v)
            for j in range(n):
                sa = sa + s_scr[j] * t_scr[slot, A, base + j:base + j + 1, :]

            def row(q, j):
                return t_scr[slot, q, pl.ds(base + j, 1), :]

            def pass2(jc, y):
                for u in range(SCAN_J_UNROLL):
                    j = jc * SCAN_J_UNROLL + u
                    s_new = s_scr[j] * row(W, j) + sa * row(B, j) + v * row(K, j)
                    s_scr[j] = s_new
                    y = y + s_new * row(R, j)
                return y

            ys.append(lax.fori_loop(0, n // SCAN_J_UNROLL, pass2, jnp.zeros_like(v)))
        _scan_store_pair(y_ref, pl.multiple_of(p * 2 * nb, 2 * nb), nb, jnp.concatenate(ys, axis=0))

    def two_pairs(pp, c):
        pair(2 * pp, 0)
        pair(2 * pp + 1, 1)
        return c

    lax.fori_loop(0, n_pairs // 2, two_pairs, 0)


def _rwkv_out_kernel(x_ref, y_ref, r_ref, k_ref, v_ref, g_ref, rk_ref, lng_ref, lnb_ref,
                     wo_ref, hsum_ref, hexp_ref, o_ref, m_scr):
    nb, tt, d = x_ref.shape
    n_blk = d // LANES
    y = y_ref[...]
    inv_n = 1.0 / HEAD_DIM
    mean = _head_sum(y, hsum_ref, hexp_ref) * inv_n
    yc = y - mean
    var = _head_sum(yc * yc, hsum_ref, hexp_ref) * inv_n
    yn = yc * lax.rsqrt(var + GN_EPS) * lng_ref[...] + lnb_ref[...]
    bonus = _head_sum(r_ref[...] * k_ref[...] * rk_ref[...], hsum_ref, hexp_ref) * v_ref[...]
    out = (yn + bonus) * g_ref[...]
    m = _bdot(out, wo_ref[...])
    for c in range(n_blk):
        m_scr[c] = m[:, c * LANES:(c + 1) * LANES]
    for b in range(nb):
        for c in range(n_blk):
            cs = slice(c * LANES, (c + 1) * LANES)
            o_ref[b, :, cs] = x_ref[b, :, cs] + m_scr[c, pl.ds(b, tt, stride=nb), :]


def _rwkv_mixer(x, gn, mu, w_rkv, w0, w1, w2, a0, a1, a2, g1, g2, k_k, k_a, r_k, ln_g, ln_b, w_o,
                batch, seq):
    t, d = x.shape
    heads = d // HEAD_DIM
    assert batch == SUBLANES and batch * heads == LANES
    tt = min(TT_RWKV, seq)
    head_of = jnp.arange(d, dtype=jnp.int32) // HEAD_DIM
    hsum = (head_of[:, None] == jnp.arange(LANES, dtype=jnp.int32)[None, :]).astype(BF16)
    hexp = hsum.T
    const = lambda *shape: pl.BlockSpec(shape, lambda i: (0,) * len(shape))
    x3 = x.reshape(batch, seq, d)
    xblk = pl.BlockSpec((batch, tt, d), lambda i: (0, i, 0))
    tok = pl.BlockSpec((tt * batch, d), lambda i: (i, 0))
    vec = lambda a: a.reshape(1, d)
    r, dec, k, v, na, b, g = pl.pallas_call(
        _rwkv_proj_kernel,
        out_shape=tuple(jax.ShapeDtypeStruct((t, d), F32) for _ in range(7)),
        grid=(seq // tt,),
        in_specs=[
            xblk,
            pl.BlockSpec((batch, SUBLANES, d), lambda i: (0, jnp.maximum(i * (tt // SUBLANES) - 1, 0), 0)),
            const(1, d), const(6, d), const(3, d, d),
            const(1, d), const(d, w1.shape[1]), const(w2.shape[0], d),
            const(1, d), const(d, a1.shape[1]), const(a2.shape[0], d),
            const(d, g1.shape[1]), const(g2.shape[0], d),
            const(1, d), const(1, d), const(d, LANES), const(LANES, d),
        ],
        out_specs=tuple(tok for _ in range(7)),
        scratch_shapes=[pltpu.VMEM((d // LANES, tt * batch, LANES), F32)],
        compiler_params=_cparams("parallel"),
        name="rwkv_proj",
    )(x3, x3, vec(gn), mu, w_rkv, vec(w0), w1, w2, vec(a0), a1, a2, g1, g2, vec(k_k), vec(k_a), hsum, hexp)

    ts = min(TT_SCAN, seq)
    blk = pl.BlockSpec((ts * batch, d), lambda i: (i, 0))
    y = pl.pallas_call(
        functools.partial(_scan_kernel, nb=batch),
        out_shape=jax.ShapeDtypeStruct((t, d), F32),
        grid=(seq // ts,),
        in_specs=[blk] * 6,
        out_specs=blk,
        scratch_shapes=[pltpu.VMEM((HEAD_DIM, HEAD_DIM, LANES), F32),
                        pltpu.VMEM((2, 6, 2 * HEAD_DIM, LANES), F32)],
        compiler_params=_cparams("arbitrary"),
        name="rwkv_scan",
    )(r, dec, k, v, na, b)

    out = pl.pallas_call(
        _rwkv_out_kernel,
        out_shape=jax.ShapeDtypeStruct((batch, seq, d), F32),
        grid=(seq // tt,),
        in_specs=[xblk] + [tok] * 5 + [const(1, d), const(1, d), const(1, d), const(d, d),
                                       const(d, LANES), const(LANES, d)],
        out_specs=xblk,
        scratch_shapes=[pltpu.VMEM((d // LANES, tt * batch, LANES), F32)],
        compiler_params=_cparams("parallel"),
        name="rwkv_out",
    )(x3, y, r, k, v, g, r_k.reshape(1, d), vec(ln_g), vec(ln_b), w_o, hsum, hexp)
    return out.reshape(t, d)


def kernel(x, norm_mix, norm_ffn, norm_final, a_w_in, a_ln_g, a_ln_b, a_w_s, a_b_s, a_w_out, b_w_in, b_w_grp, b_scale, b_w_out, c_mu, c_w_rkv, c_w0, c_w1, c_w2, c_a0, c_a1, c_a2, c_g1, c_g2, c_k_k, c_k_a, c_r_k, c_ln_g, c_ln_b, c_w_o, f_w_gate, f_w_up, f_w_down, m_router, m_w_gate, m_w_up, m_w_down):
    batch, seq, d = x.shape
    depth = norm_mix.shape[0]
    bf = lambda w: w.astype(BF16)
    xt = x.reshape(batch * seq, d)
    f_w = (bf(f_w_gate), bf(f_w_up), bf(f_w_down))
    m_w = (m_w_gate, m_w_up, m_w_down)
    for layer in range(depth):
        kind, j = layer % 3, layer // 3
        if kind == 0:
            xt = _gmlp_mixer(xt, norm_mix[layer], bf(a_w_in[j]), a_ln_g[j], a_ln_b[j], a_w_s[j], a_b_s[j],
                             bf(a_w_out[j]))
        elif kind == 1:
            xt = _pool_mixer(xt, norm_mix[layer], bf(b_w_in[j]), bf(b_w_grp[j]), b_scale[j], bf(b_w_out[j]), seq)
        else:
            xt = _rwkv_mixer(xt, norm_mix[layer], c_mu[j], bf(c_w_rkv[j]), c_w0[j], bf(c_w1[j]), bf(c_w2[j]),
                             c_a0[j], bf(c_a1[j]), bf(c_a2[j]), bf(c_g1[j]), bf(c_g2[j]),
                             c_k_k[j], c_k_a[j], c_r_k[j], c_ln_g[j], c_ln_b[j], bf(c_w_o[j]), batch, seq)
        j = layer // 2
        last = layer == depth - 1
        if layer % 2 == 0:
            xt = _ffn_dense(xt, norm_ffn[layer], *f_w, j)
            if last:
                xt = _final_norm(xt, norm_final)
        else:
            xt = _moe_layer(xt, norm_ffn[layer], m_router[j], *m_w, j, norm_final, last)
    return xt.reshape(batch, seq, d)


def _final_norm_kernel(x_ref, gn_ref, o_ref):
    o_ref[...] = _rms(x_ref[...], gn_ref[...])


def _final_norm(x, gn):
    t, d = x.shape
    tm = min(TM_FFN, t)
    return pl.pallas_call(
        _final_norm_kernel,
        out_shape=jax.ShapeDtypeStruct((t, d), F32),
        grid=(t // tm,),
        in_specs=[pl.BlockSpec((tm, d), lambda i: (i, 0)), pl.BlockSpec((1, d), lambda i: (0, 0))],
        out_specs=pl.BlockSpec((tm, d), lambda i: (i, 0)),
        compiler_params=_cparams("parallel"),
        name="final_norm",
    )(x, gn.reshape(1, d))
```

```python
import functools

import jax
import jax.numpy as jnp
from jax import lax
from jax.experimental import pallas as pl
from jax.experimental.pallas import tpu as pltpu

F32 = jnp.float32
BF16 = jnp.bfloat16

RMS_EPS = 1e-6
LN_EPS = 1e-5
GN_EPS = 64e-5

CHUNK = 128
A_GROUPS = 8
POOL_WINDOWS = (2, 4, 8, 16)
POOL_HALO = 16
HEAD_DIM = 64
N_EXPERTS = 8
TOP_K = 2

LANES = 128
SUBLANES = 8
VMEM_LIMIT = 56 * 1024 * 1024

TM_GMLP = 1024
TM_FFN = 1024
TF_FFN = 512
TM_POOL = 1024
TM_ROUTER = 1024
TB_MOE = 512
TM_MOE = 1024
MOE_SUB = 512
DMA_UNROLL = 8
TT_RWKV = 32
TT_SCAN = 64
SCAN_J_UNROLL = 8


def _cparams(*sem):
    return pltpu.CompilerParams(dimension_semantics=sem, vmem_limit_bytes=VMEM_LIMIT)


def _rms(x, g):
    return x * lax.rsqrt(jnp.mean(x * x, axis=-1, keepdims=True) + RMS_EPS) * g


def _bdot(a, b):
    return jnp.dot(a.astype(BF16), b.astype(BF16), preferred_element_type=F32)


def _split_dot(a, b):
    hi = a.astype(BF16)
    lo = (a - hi.astype(F32)).astype(BF16)
    return (jnp.dot(hi, b, preferred_element_type=F32)
            + jnp.dot(lo, b, preferred_element_type=F32))


def _gmlp_kernel(x_ref, gn_ref, win_ref, lng_ref, lnb_ref, ws_ref, bs_ref, wout_ref,
                 o_ref, us_scr, *, width, n_chunks):
    x = x_ref[...]
    h = _rms(x, gn_ref[...])
    z = _bdot(h, win_ref[...])
    z = 0.5 * z * (1.0 + lax.erf(z * (2.0 ** -0.5)))
    u = z[:, :width]
    v = z[:, width:]
    mu = jnp.mean(v, axis=-1, keepdims=True)
    vc = v - mu
    var = jnp.mean(vc * vc, axis=-1, keepdims=True)
    vb = (vc * lax.rsqrt(var + LN_EPS) * lng_ref[...] + lnb_ref[...]).astype(BF16)
    gd = width // A_GROUPS
    row = lax.broadcasted_iota(jnp.int32, (CHUNK, CHUNK), 0)
    col = lax.broadcasted_iota(jnp.int32, (CHUNK, CHUNK), 1)
    causal = row >= col
    for g in range(A_GROUPS):
        w = jnp.where(causal, ws_ref[g], 0.0).astype(BF16)
        rhs = jnp.concatenate(
            [vb[c * CHUNK:(c + 1) * CHUNK, g * gd:(g + 1) * gd] for c in range(n_chunks)], axis=1)
        s = jnp.dot(w, rhs, preferred_element_type=F32)
        bias = bs_ref[g]
        for c in range(n_chunks):
            sc = s[:, c * gd:(c + 1) * gd] + bias
            uc = u[c * CHUNK:(c + 1) * CHUNK, g * gd:(g + 1) * gd]
            us_scr[c * CHUNK:(c + 1) * CHUNK, g * gd:(g + 1) * gd] = (uc * sc).astype(BF16)
    o_ref[...] = x + jnp.dot(us_scr[...], wout_ref[...].astype(BF16), preferred_element_type=F32)


def _gmlp_mixer(x, gn, w_in, ln_g, ln_b, w_s, b_s, w_out):
    t, d = x.shape
    width = w_in.shape[1] // 2
    gd = width // A_GROUPS
    tm = min(TM_GMLP, t)
    n_chunks = tm // CHUNK
    bias = jnp.broadcast_to(b_s[:, :, None], (A_GROUPS, CHUNK, gd))
    const = lambda *shape: pl.BlockSpec(shape, lambda i: (0,) * len(shape))
    return pl.pallas_call(
        functools.partial(_gmlp_kernel, width=width, n_chunks=n_chunks),
        out_shape=jax.ShapeDtypeStruct((t, d), F32),
        grid=(t // tm,),
        in_specs=[
            pl.BlockSpec((tm, d), lambda i: (i, 0)),
            const(1, d), const(d, 2 * width), const(1, width), const(1, width),
            const(A_GROUPS, CHUNK, CHUNK), const(A_GROUPS, CHUNK, gd), const(width, d),
        ],
        out_specs=pl.BlockSpec((tm, d), lambda i: (i, 0)),
        scratch_shapes=[pltpu.VMEM((tm, width), BF16)],
        compiler_params=_cparams("parallel"),
        name="gmlp_mixer",
    )(x, gn.reshape(1, d), w_in, ln_g.reshape(1, width), ln_b.reshape(1, width), w_s, bias, w_out)


def _pool_kernel(x_ref, xh_ref, gn_ref, win_ref, wgrp_ref, scale_ref, wout_ref, o_ref, d_scr,
                 *, seq, tm):
    i = pl.program_id(0)
    x = x_ref[...]
    width = win_ref.shape[1]
    gd = width // len(POOL_WINDOWS)
    t0 = (i * tm) % seq
    xa = jnp.concatenate([xh_ref[...], x], axis=0)
    p_all = _bdot(_rms(xa, gn_ref[...]), win_ref[...])
    r = lax.broadcasted_iota(jnp.int32, (POOL_HALO + tm, 1), 0)
    p_all = jnp.where(r + (t0 - POOL_HALO) >= 0, p_all, 0.0)
    pos = lax.broadcasted_iota(jnp.int32, (tm, 1), 0) + (t0 + 1)

    sums = p_all
    have = 1
    for gi, win in enumerate(POOL_WINDOWS):
        while have < win:
            sums = sums[have:] + sums[:-have]
            have *= 2
        lo, hi = gi * gd, (gi + 1) * gd
        s = sums[POOL_HALO - (win - 1):POOL_HALO - (win - 1) + tm, lo:hi]
        count = jnp.minimum(pos, win).astype(F32)
        d_scr[:, lo:hi] = (s / count - p_all[POOL_HALO:, lo:hi]).astype(BF16)
    ys = []
    for gi in range(len(POOL_WINDOWS)):
        lo, hi = gi * gd, (gi + 1) * gd
        ys.append(jnp.dot(d_scr[:, lo:hi], wgrp_ref[gi].astype(BF16), preferred_element_type=F32))
    y = jnp.concatenate(ys, axis=1) * scale_ref[...]
    o_ref[...] = x + _bdot(y, wout_ref[...])


def _pool_mixer(x, gn, w_in, w_grp, scale, w_out, seq):
    t, d = x.shape
    width = w_in.shape[1]
    ng = len(POOL_WINDOWS)
    gd = width // ng
    tm = min(TM_POOL, seq)
    hb = tm // POOL_HALO
    const = lambda *shape: pl.BlockSpec(shape, lambda i: (0,) * len(shape))
    return pl.pallas_call(
        functools.partial(_pool_kernel, seq=seq, tm=tm),
        out_shape=jax.ShapeDtypeStruct((t, d), F32),
        grid=(t // tm,),
        in_specs=[
            pl.BlockSpec((tm, d), lambda i: (i, 0)),
            pl.BlockSpec((POOL_HALO, d), lambda i: (jnp.maximum(i * hb - 1, 0), 0)),
            const(1, d), const(d, width), const(ng, gd, gd), const(1, width), const(width, d),
        ],
        out_specs=pl.BlockSpec((tm, d), lambda i: (i, 0)),
        scratch_shapes=[pltpu.VMEM((tm, width), BF16)],
        compiler_params=_cparams("parallel"),
        name="pool_mixer",
    )(x, x, gn.reshape(1, d), w_in, w_grp, scale.reshape(1, width), w_out)


def _ffn_kernel(x_ref, gn_ref, wg_ref, wu_ref, wd_ref, o_ref, h_scr, acc_scr):
    j = pl.program_id(1)

    @pl.when(j == 0)
    def _():
        h_scr[...] = _rms(x_ref[...], gn_ref[...]).astype(BF16)
        acc_scr[...] = jnp.zeros_like(acc_scr)

    h = h_scr[...]
    gate = jnp.dot(h, wg_ref[...].astype(BF16), preferred_element_type=F32)
    up = jnp.dot(h, wu_ref[...].astype(BF16), preferred_element_type=F32)
    act = (gate * jax.nn.sigmoid(gate) * up).astype(BF16)
    acc_scr[...] += jnp.dot(act, wd_ref[...].astype(BF16), preferred_element_type=F32)

    @pl.when(j == pl.num_programs(1) - 1)
    def _():
        o_ref[...] = x_ref[...] + acc_scr[...]


def _ffn_dense(x, gn, w_gate, w_up, w_down, layer):
    t, d = x.shape
    f = w_gate.shape[2]
    tm = min(TM_FFN, t)
    tf = min(TF_FFN, f)
    return pl.pallas_call(
        _ffn_kernel,
        out_shape=jax.ShapeDtypeStruct((t, d), F32),
        grid=(t // tm, f // tf),
        in_specs=[
            pl.BlockSpec((tm, d), lambda i, j: (i, 0)),
            pl.BlockSpec((1, d), lambda i, j: (0, 0)),
            pl.BlockSpec((None, d, tf), lambda i, j: (layer, 0, j)),
            pl.BlockSpec((None, d, tf), lambda i, j: (layer, 0, j)),
            pl.BlockSpec((None, tf, d), lambda i, j: (layer, j, 0)),
        ],
        out_specs=pl.BlockSpec((tm, d), lambda i, j: (i, 0)),
        scratch_shapes=[pltpu.VMEM((tm, d), BF16), pltpu.VMEM((tm, d), F32)],
        compiler_params=_cparams("parallel", "arbitrary"),
        name="ffn_dense",
    )(x, gn.reshape(1, d), w_gate, w_up, w_down)


def _router_kernel(x_ref, gn_ref, wr_ref, idx_ref, gate_ref):
    h = _rms(x_ref[...], gn_ref[...])
    wr = wr_ref[...]
    w_hi = wr.astype(BF16)
    w_lo = (wr - w_hi.astype(F32)).astype(BF16)
    logits = _split_dot(h, w_hi) + jnp.dot(h.astype(BF16), w_lo, preferred_element_type=F32)
    lane = lax.broadcasted_iota(jnp.int32, logits.shape, 1)
    lane_f = lane.astype(F32)
    neg = jnp.float32(-jnp.inf)
    logits = jnp.where(lane < N_EXPERTS, logits, neg)
    m1 = jnp.max(logits, axis=-1, keepdims=True)
    i1 = jnp.min(jnp.where(logits == m1, lane_f, float(LANES)), axis=-1, keepdims=True)
    rest = jnp.where(lane_f == i1, neg, logits)
    m2 = jnp.max(rest, axis=-1, keepdims=True)
    i2 = jnp.min(jnp.where(rest == m2, lane_f, float(LANES)), axis=-1, keepdims=True)
    e = jnp.exp(m2 - m1)
    den = 1.0 + e
    idx_ref[...] = jnp.where(lane == 0, i1, jnp.where(lane == 1, i2, 0.0)).astype(jnp.int32)
    gate_ref[...] = jnp.where(lane == 0, 1.0 / den, jnp.where(lane == 1, e / den, 0.0))


def _moe_router(x, gn, router):
    t, d = x.shape
    tm = min(TM_ROUTER, t)
    wr = jnp.zeros((d, LANES), F32).at[:, :N_EXPERTS].set(router)
    return pl.pallas_call(
        _router_kernel,
        out_shape=(jax.ShapeDtypeStruct((t, LANES), jnp.int32), jax.ShapeDtypeStruct((t, LANES), F32)),
        grid=(t // tm,),
        in_specs=[
            pl.BlockSpec((tm, d), lambda i: (i, 0)),
            pl.BlockSpec((1, d), lambda i: (0, 0)),
            pl.BlockSpec((d, LANES), lambda i: (0, 0)),
        ],
        out_specs=(pl.BlockSpec((tm, LANES), lambda i: (i, 0)), pl.BlockSpec((tm, LANES), lambda i: (i, 0))),
        compiler_params=_cparams("parallel"),
        name="moe_router",
    )(x, gn.reshape(1, d), wr)


def _tok_copy(src, src_tok, dst, dst_tok, sem):
    return pltpu.make_async_copy(
        src.at[pl.ds(pl.multiple_of(src_tok * SUBLANES, SUBLANES), SUBLANES)],
        dst.at[pl.ds(pl.multiple_of(dst_tok * SUBLANES, SUBLANES), SUBLANES)], sem)


def _to_tiles(ref, base, val):
    n, d = val.shape
    for c in range(d // LANES):
        ref[pl.ds(base + c, n, stride=SUBLANES), :] = val[:, c * LANES:(c + 1) * LANES]


def _from_tiles(ref, base, n, d):
    return jnp.concatenate(
        [ref[pl.ds(base + c, n, stride=SUBLANES), :] for c in range(d // LANES)], axis=1)


def _dispatch_kernel(dest_ref, nv_ref, x_ref, gn_ref, buf_ref, h_scr, sem, *, tm):
    n = x_ref.shape[0]

    @pl.when(pl.program_id(0) == 0)
    def _():
        h_scr[...] = jnp.zeros_like(h_scr)

        def fills(blk):
            return [pltpu.make_async_copy(
                h_scr, buf_ref.at[pl.ds((blk * tm + q * n) * SUBLANES, n * SUBLANES)], sem)
                for q in range(tm // n)]

        for blk in range(nv_ref.shape[0]):
            @pl.when(nv_ref[blk] < tm)
            def _():
                for q, cp in enumerate(fills(blk)):
                    cp.start(priority=q % 2)

        for blk in range(nv_ref.shape[0]):
            @pl.when(nv_ref[blk] < tm)
            def _():
                for cp in fills(blk):
                    cp.wait()

    _to_tiles(h_scr, 0, _rms(x_ref[...], gn_ref[...]))

    def copies(r):
        return [_tok_copy(h_scr, r, buf_ref, dest_ref[TOP_K * r + k], sem) for k in range(TOP_K)]

    def start(g, c):
        for u in range(DMA_UNROLL):
            for k, cp in enumerate(copies(g * DMA_UNROLL + u)):
                cp.start(priority=k % 2)
        return c

    lax.fori_loop(0, n // DMA_UNROLL, start, 0)

    def wait(g, c):
        for u in range(DMA_UNROLL):
            for cp in copies(g * DMA_UNROLL + u):
                cp.wait()
        return c

    lax.fori_loop(0, n // DMA_UNROLL, wait, 0)


def _moe_dispatch(x, gn, dest, n_valid, n_rows, tm):
    t, d = x.shape
    tb = min(TB_MOE, t)
    tile_rows = d // LANES
    assert tile_rows == SUBLANES and tm % tb == 0
    return pl.pallas_call(
        functools.partial(_dispatch_kernel, tm=tm),
        out_shape=jax.ShapeDtypeStruct((n_rows * tile_rows, LANES), F32),
        grid=(t // tb,),
        in_specs=[
            pl.BlockSpec((TOP_K * tb,), lambda i: (i,), memory_space=pltpu.SMEM),
            pl.BlockSpec(memory_space=pltpu.SMEM),
            pl.BlockSpec((tb, d), lambda i: (i, 0)),
            pl.BlockSpec((1, d), lambda i: (0, 0)),
        ],
        out_specs=pl.BlockSpec(memory_space=pl.ANY),
        scratch_shapes=[pltpu.VMEM((tb * tile_rows, LANES), F32), pltpu.SemaphoreType.DMA],
        compiler_params=_cparams("arbitrary"),
        name="moe_dispatch",
    )(dest, n_valid, x, gn.reshape(1, d))


def _moe_ffn_kernel(be_ref, nv_ref, xg_ref, wg_ref, wu_ref, wd_ref, o_ref, h_scr, acc_scr):
    i = pl.program_id(0)
    j = pl.program_id(1)
    tm, d = h_scr.shape
    nv = nv_ref[i]

    @pl.when(j == 0)
    def _():
        h_scr[...] = _from_tiles(xg_ref, 0, tm, d).astype(BF16)
        acc_scr[...] = jnp.zeros_like(acc_scr)

    def swiglu_rows(rs):
        h = h_scr[rs, :]
        gate = jnp.dot(h, wg_ref[...].astype(BF16), preferred_element_type=F32)
        up = jnp.dot(h, wu_ref[...].astype(BF16), preferred_element_type=F32)
        act = (gate * jax.nn.sigmoid(gate) * up).astype(BF16)
        acc_scr[rs, :] += jnp.dot(act, wd_ref[...].astype(BF16), preferred_element_type=F32)

    @pl.when(nv > tm - MOE_SUB)
    def _():
        swiglu_rows(slice(0, tm))

    @pl.when(jnp.logical_and(nv > 0, nv <= tm - MOE_SUB))
    def _():
        for s in range(tm // MOE_SUB - 1):
            @pl.when(nv > s * MOE_SUB)
            def _():
                swiglu_rows(slice(s * MOE_SUB, (s + 1) * MOE_SUB))

    @pl.when(j == pl.num_programs(1) - 1)
    def _():
        _to_tiles(o_ref, 0, acc_scr[...])


def _moe_ffn(buf, block_expert, n_valid, w_gate, w_up, w_down, layer, d):
    n_rows = buf.shape[0] // SUBLANES
    f = w_gate.shape[3]
    tm = TM_MOE
    tf = min(TF_FFN, f)
    nb, nf = n_rows // tm, f // tf

    def fj(i, j, nv):
        return jnp.where(nv[i] > 0, j, nf - 1)

    grid_spec = pltpu.PrefetchScalarGridSpec(
        num_scalar_prefetch=2,
        grid=(nb, nf),
        in_specs=[
            pl.BlockSpec((tm * SUBLANES, LANES), lambda i, j, be, nv: (i, 0)),
            pl.BlockSpec((None, None, d, tf), lambda i, j, be, nv: (layer, be[i], 0, fj(i, j, nv))),
            pl.BlockSpec((None, None, d, tf), lambda i, j, be, nv: (layer, be[i], 0, fj(i, j, nv))),
            pl.BlockSpec((None, None, tf, d), lambda i, j, be, nv: (layer, be[i], fj(i, j, nv), 0)),
        ],
        out_specs=pl.BlockSpec((tm * SUBLANES, LANES), lambda i, j, be, nv: (i, 0)),
        scratch_shapes=[pltpu.VMEM((tm, d), BF16), pltpu.VMEM((tm, d), F32)],
    )
    return pl.pallas_call(
        _moe_ffn_kernel,
        out_shape=jax.ShapeDtypeStruct(buf.shape, F32),
        grid_spec=grid_spec,
        compiler_params=_cparams("arbitrary", "arbitrary"),
        name="moe_ffn",
    )(block_expert, n_valid, buf, w_gate, w_up, w_down)


def _combine_kernel(dest_ref, dest_next_ref, x_ref, gate_ref, gn_ref, y_ref, o_ref, rows_scr, sem, *,
                    final_norm):
    n, d = x_ref.shape
    i = pl.program_id(0)
    slot = i % 2
    slot_toks = TOP_K * n

    def copies(dref, sl, r):
        return [_tok_copy(y_ref, dref[TOP_K * r + k], rows_scr, sl * slot_toks + k * n + r, sem.at[sl])
                for k in range(TOP_K)]

    def gather(dref, sl):
        def start(g, c):
            for u in range(DMA_UNROLL):
                for k, cp in enumerate(copies(dref, sl, g * DMA_UNROLL + u)):
                    cp.start(priority=k % 2)
            return c

        lax.fori_loop(0, n // DMA_UNROLL, start, 0)

    @pl.when(i == 0)
    def _():
        gather(dest_ref, slot)

    @pl.when(i + 1 < pl.num_programs(0))
    def _():
        gather(dest_next_ref, 1 - slot)

    def wait(g, c):
        for u in range(DMA_UNROLL):
            for cp in copies(dest_ref, slot, g * DMA_UNROLL + u):
                cp.wait()
        return c

    lax.fori_loop(0, n // DMA_UNROLL, wait, 0)
    gates = gate_ref[...]
    out = x_ref[...]
    for k in range(TOP_K):
        out = out + _from_tiles(rows_scr, (slot * slot_toks + k * n) * SUBLANES, n, d) * gates[:, k:k + 1]
    if final_norm:
        out = _rms(out, gn_ref[...])
    o_ref[...] = out


def _moe_combine(x, y_grouped, dest, gates, gn_final, final_norm):
    t, d = x.shape
    tb = min(TB_MOE, t)
    n_steps = t // tb
    return pl.pallas_call(
        functools.partial(_combine_kernel, final_norm=final_norm),
        out_shape=jax.ShapeDtypeStruct((t, d), F32),
        grid=(n_steps,),
        in_specs=[
            pl.BlockSpec((TOP_K * tb,), lambda i: (i,), memory_space=pltpu.SMEM),
            pl.BlockSpec((TOP_K * tb,), lambda i: (jnp.minimum(i + 1, n_steps - 1),), memory_space=pltpu.SMEM),
            pl.BlockSpec((tb, d), lambda i: (i, 0)),
            pl.BlockSpec((tb, LANES), lambda i: (i, 0)),
            pl.BlockSpec((1, d), lambda i: (0, 0)),
            pl.BlockSpec(memory_space=pl.ANY),
        ],
        out_specs=pl.BlockSpec((tb, d), lambda i: (i, 0)),
        scratch_shapes=[pltpu.VMEM((2 * TOP_K * tb * SUBLANES, LANES), F32), pltpu.SemaphoreType.DMA((2,))],
        compiler_params=_cparams("arbitrary"),
        name="moe_combine",
    )(dest, dest, x, gates, gn_final.reshape(1, d), y_grouped)


def _moe_layer(x, gn, router, w_gate, w_up, w_down, layer, gn_final, final_norm):
    t, d = x.shape
    tm = TM_MOE
    idx, gates = _moe_router(x, gn, router)
    flat_e = idx[:, :TOP_K].reshape(-1)
    onehot = (flat_e[:, None] == jnp.arange(N_EXPERTS, dtype=jnp.int32)[None, :]).astype(jnp.int32)
    csum = jnp.cumsum(onehot, axis=0)
    rank = jnp.sum(csum * onehot, axis=1) - 1
    counts = csum[-1]
    padded = (counts + tm - 1) // tm * tm
    pad_end = jnp.cumsum(padded)
    pad_start = pad_end - padded
    dest = (jnp.sum(pad_start[None, :] * onehot, axis=1) + rank).astype(jnp.int32)
    nb = -(-(t * TOP_K) // tm) + N_EXPERTS
    block_start = jnp.arange(nb, dtype=jnp.int32) * tm
    block_expert = jnp.minimum(
        jnp.sum((block_start[:, None] >= pad_end[None, :]).astype(jnp.int32), axis=1), N_EXPERTS - 1)
    n_valid = jnp.clip(counts[block_expert] - (block_start - pad_start[block_expert]), 0, tm).astype(jnp.int32)
    buf = _moe_dispatch(x, gn, dest, n_valid, nb * tm, tm)
    y_grouped = _moe_ffn(buf, block_expert.astype(jnp.int32), n_valid, w_gate, w_up, w_down, layer, d)
    return _moe_combine(x, y_grouped, dest, gates, gn_final, final_norm)


def _head_sum(a, hsum_ref, hexp_ref):
    return _split_dot(_split_dot(a, hsum_ref[...]), hexp_ref[...])


def _rwkv_proj_kernel(x_ref, xh_ref, gn_ref, mu_ref, wrkv_ref, w0_ref, w1_ref, w2_ref,
                      a0_ref, a1_ref, a2_ref, g1_ref, g2_ref, kk_ref, ka_ref, hsum_ref, hexp_ref,
                      r_ref, dec_ref, k_ref, v_ref, na_ref, b_ref, g_ref, x_scr):
    i = pl.program_id(0)
    nb, tt, d = x_ref.shape
    n_blk = d // LANES
    for b in range(nb):
        for c in range(n_blk):
            x_scr[c, pl.ds(b, tt, stride=nb), :] = x_ref[b, :, c * LANES:(c + 1) * LANES]
    gn = gn_ref[...]
    h = _rms(jnp.concatenate([x_scr[c] for c in range(n_blk)], axis=1), gn)
    x_last = jnp.concatenate([xh_ref[b, SUBLANES - 1:SUBLANES, :] for b in range(nb)], axis=0)
    h_last = jnp.where(i == 0, 0.0, _rms(x_last, gn))
    h_prev = jnp.concatenate([h_last, h[:-nb]], axis=0)
    xx = h_prev - h
    mu = mu_ref[...]
    xs = [h + xx * mu[n:n + 1, :] for n in range(6)]
    r = _bdot(xs[0], wrkv_ref[0])
    k = _bdot(xs[1], wrkv_ref[1])
    v = _bdot(xs[2], wrkv_ref[2])
    wl = w0_ref[...] + _bdot(jnp.tanh(_bdot(xs[3], w1_ref[...])), w2_ref[...])
    w = -jax.nn.softplus(-wl) - 0.5
    a = jax.nn.sigmoid(a0_ref[...] + _bdot(_bdot(xs[4], a1_ref[...]), a2_ref[...]))
    g = _bdot(jax.nn.sigmoid(_bdot(xs[5], g1_ref[...])), g2_ref[...])
    kk = k * kk_ref[...]
    norm = jnp.sqrt(_head_sum(kk * kk, hsum_ref, hexp_ref))
    kk = kk / jnp.maximum(norm, 1e-12)
    r_ref[...] = r
    dec_ref[...] = jnp.exp(-jnp.exp(w))
    k_ref[...] = k * (1.0 + (a - 1.0) * ka_ref[...])
    v_ref[...] = v
    na_ref[...] = -kk
    b_ref[...] = kk * a
    g_ref[...] = g


def _half_swap(a0, a1):
    lo = lax.broadcasted_iota(jnp.int32, a0.shape, 1) < LANES // 2
    return (jnp.where(lo, a0, pltpu.roll(a1, LANES // 2, 1)),
            jnp.where(lo, pltpu.roll(a0, LANES // 2, 1), a1))


def _scan_load_pair(ref, row0, nb):
    p0 = ref[pl.ds(row0, nb), :]
    p1 = ref[pl.ds(row0 + nb, nb), :]
    n_blk = p0.shape[1] // LANES
    even, odd = [], []
    for c in range(n_blk):
        e, o = _half_swap(p0[:, c * LANES:(c + 1) * LANES], p1[:, c * LANES:(c + 1) * LANES])
        even.append(e)
        odd.append(o)
    return jnp.concatenate(even + odd, axis=0).T


def _scan_store_pair(ref, row0, nb, y_pair):
    yt = y_pair.T
    n_blk = ref.shape[1] // LANES
    for c in range(n_blk):
        y0, y1 = _half_swap(yt[c * nb:(c + 1) * nb], yt[(n_blk + c) * nb:(n_blk + c + 1) * nb])
        ref[pl.ds(row0, nb), c * LANES:(c + 1) * LANES] = y0
        ref[pl.ds(row0 + nb, nb), c * LANES:(c + 1) * LANES] = y1


def _scan_kernel(r_ref, w_ref, k_ref, v_ref, a_ref, b_ref, y_ref, s_scr, t_scr, *, nb):
    n = s_scr.shape[0]
    refs = (r_ref, w_ref, k_ref, v_ref, a_ref, b_ref)
    R, W, K, V, A, B = range(6)
    n_pairs = r_ref.shape[0] // (2 * nb)

    @pl.when(pl.program_id(0) == 0)
    def _():
        s_scr[...] = jnp.zeros_like(s_scr)

    def load_pair(p, slot, which):
        row0 = pl.multiple_of(p * 2 * nb, 2 * nb)
        for q in which:
            t_scr[slot, q] = _scan_load_pair(refs[q], row0, nb)

    load_pair(0, 0, range(6))

    def pair(p, slot):
        p_next = jnp.minimum(p + 1, n_pairs - 1)
        ys = []
        for t2 in range(2):
            base = t2 * n
            load_pair(p_next, 1 - slot, range(3 * t2, 3 * t2 + 3))
            v = t_scr[slot, V, base:base + n, :]
            sa = jnp.zeros_like(v)
            for j in range(n):
                sa = sa + s_scr[j] * t_scr[slot, A, base + j:base + j + 1, :]

            def row(q, j):
                return t_scr[slot, q, pl.ds(base + j, 1), :]

            def pass2(jc, y):
                for u in range(SCAN_J_UNROLL):
                    j = jc * SCAN_J_UNROLL + u
                    s_new = s_scr[j] * row(W, j) + sa * row(B, j) + v * row(K, j)
                    s_scr[j] = s_new
                    y = y + s_new * row(R, j)
                return y

            ys.append(lax.fori_loop(0, n // SCAN_J_UNROLL, pass2, jnp.zeros_like(v)))
        _scan_store_pair(y_ref, pl.multiple_of(p * 2 * nb, 2 * nb), nb, jnp.concatenate(ys, axis=0))

    def two_pairs(pp, c):
        pair(2 * pp, 0)
        pair(2 * pp + 1, 1)
        return c

    lax.fori_loop(0, n_pairs // 2, two_pairs, 0)


def _rwkv_out_kernel(x_ref, y_ref, r_ref, k_ref, v_ref, g_ref, rk_ref, lng_ref, lnb_ref,
                     wo_ref, hsum_ref, hexp_ref, o_ref, m_scr):
    nb, tt, d = x_ref.shape
    n_blk = d // LANES
    y = y_ref[...]
    inv_n = 1.0 / HEAD_DIM
    mean = _head_sum(y, hsum_ref, hexp_ref) * inv_n
    yc = y - mean
    var = _head_sum(yc * yc, hsum_ref, hexp_ref) * inv_n
    yn = yc * lax.rsqrt(var + GN_EPS) * lng_ref[...] + lnb_ref[...]
    bonus = _head_sum(r_ref[...] * k_ref[...] * rk_ref[...], hsum_ref, hexp_ref) * v_ref[...]
    out = (yn + bonus) * g_ref[...]
    m = _bdot(out, wo_ref[...])
    for c in range(n_blk):
        m_scr[c] = m[:, c * LANES:(c + 1) * LANES]
    for b in range(nb):
        for c in range(n_blk):
            cs = slice(c * LANES, (c + 1) * LANES)
            o_ref[b, :, cs] = x_ref[b, :, cs] + m_scr[c, pl.ds(b, tt, stride=nb), :]


def _rwkv_mixer(x, gn, mu, w_rkv, w0, w1, w2, a0, a1, a2, g1, g2, k_k, k_a, r_k, ln_g, ln_b, w_o,
                batch, seq):
    t, d = x.shape
    heads = d // HEAD_DIM
    assert batch == SUBLANES and batch * heads == LANES
    tt = min(TT_RWKV, seq)
    head_of = jnp.arange(d, dtype=jnp.int32) // HEAD_DIM
    hsum = (head_of[:, None] == jnp.arange(LANES, dtype=jnp.int32)[None, :]).astype(BF16)
    hexp = hsum.T
    const = lambda *shape: pl.BlockSpec(shape, lambda i: (0,) * len(shape))
    x3 = x.reshape(batch, seq, d)
    xblk = pl.BlockSpec((batch, tt, d), lambda i: (0, i, 0))
    tok = pl.BlockSpec((tt * batch, d), lambda i: (i, 0))
    vec = lambda a: a.reshape(1, d)
    r, dec, k, v, na, b, g = pl.pallas_call(
        _rwkv_proj_kernel,
        out_shape=tuple(jax.ShapeDtypeStruct((t, d), F32) for _ in range(7)),
        grid=(seq // tt,),
        in_specs=[
            xblk,
            pl.BlockSpec((batch, SUBLANES, d), lambda i: (0, jnp.maximum(i * (tt // SUBLANES) - 1, 0), 0)),
            const(1, d), const(6, d), const(3, d, d),
            const(1, d), const(d, w1.shape[1]), const(w2.shape[0], d),
            const(1, d), const(d, a1.shape[1]), const(a2.shape[0], d),
            const(d, g1.shape[1]), const(g2.shape[0], d),
            const(1, d), const(1, d), const(d, LANES), const(LANES, d),
        ],
        out_specs=tuple(tok for _ in range(7)),
        scratch_shapes=[pltpu.VMEM((d // LANES, tt * batch, LANES), F32)],
        compiler_params=_cparams("parallel"),
        name="rwkv_proj",
    )(x3, x3, vec(gn), mu, w_rkv, vec(w0), w1, w2, vec(a0), a1, a2, g1, g2, vec(k_k), vec(k_a), hsum, hexp)

    ts = min(TT_SCAN, seq)
    blk = pl.BlockSpec((ts * batch, d), lambda i: (i, 0))
    y = pl.pallas_call(
        functools.partial(_scan_kernel, nb=batch),
        out_shape=jax.ShapeDtypeStruct((t, d), F32),
        grid=(seq // ts,),
        in_specs=[blk] * 6,
        out_specs=blk,
        scratch_shapes=[pltpu.VMEM((HEAD_DIM, HEAD_DIM, LANES), F32),
                        pltpu.VMEM((2, 6, 2 * HEAD_DIM, LANES), F32)],
        compiler_params=_cparams("arbitrary"),
        name="rwkv_scan",
    )(r, dec, k, v, na, b)

    out = pl.pallas_call(
        _rwkv_out_kernel,
        out_shape=jax.ShapeDtypeStruct((batch, seq, d), F32),
        grid=(seq // tt,),
        in_specs=[xblk] + [tok] * 5 + [const(1, d), const(1, d), const(1, d), const(d, d),
                                       const(d, LANES), const(LANES, d)],
        out_specs=xblk,
        scratch_shapes=[pltpu.VMEM((d // LANES, tt * batch, LANES), F32)],
        compiler_params=_cparams("parallel"),
        name="rwkv_out",
    )(x3, y, r, k, v, g, r_k.reshape(1, d), vec(ln_g), vec(ln_b), w_o, hsum, hexp)
    return out.reshape(t, d)


def kernel(x, norm_mix, norm_ffn, norm_final, a_w_in, a_ln_g, a_ln_b, a_w_s, a_b_s, a_w_out, b_w_in, b_w_grp, b_scale, b_w_out, c_mu, c_w_rkv, c_w0, c_w1, c_w2, c_a0, c_a1, c_a2, c_g1, c_g2, c_k_k, c_k_a, c_r_k, c_ln_g, c_ln_b, c_w_o, f_w_gate, f_w_up, f_w_down, m_router, m_w_gate, m_w_up, m_w_down):
    batch, seq, d = x.shape
    depth = norm_mix.shape[0]
    bf = lambda w: w.astype(BF16)
    xt = x.reshape(batch * seq, d)
    f_w = (bf(f_w_gate), bf(f_w_up), bf(f_w_down))
    m_w = (m_w_gate, m_w_up, m_w_down)
    for layer in range(depth):
        kind, j = layer % 3, layer // 3
        if kind == 0:
            xt = _gmlp_mixer(xt, norm_mix[layer], bf(a_w_in[j]), a_ln_g[j], a_ln_b[j], a_w_s[j], a_b_s[j],
                             bf(a_w_out[j]))
        elif kind == 1:
            xt = _pool_mixer(xt, norm_mix[layer], bf(b_w_in[j]), bf(b_w_grp[j]), b_scale[j], bf(b_w_out[j]), seq)
        else:
            xt = _rwkv_mixer(xt, norm_mix[layer], c_mu[j], bf(c_w_rkv[j]), c_w0[j], bf(c_w1[j]), bf(c_w2[j]),
                             c_a0[j], bf(c_a1[j]), bf(c_a2[j]), bf(c_g1[j]), bf(c_g2[j]),
                             c_k_k[j], c_k_a[j], c_r_k[j], c_ln_g[j], c_ln_b[j], bf(c_w_o[j]), batch, seq)
        j = layer // 2
        last = layer == depth - 1
        if layer % 2 == 0:
            xt = _ffn_dense(xt, norm_ffn[layer], *f_w, j)
            if last:
                xt = _final_norm(xt, norm_final)
        else:
            xt = _moe_layer(xt, norm_ffn[layer], m_router[j], *m_w, j, norm_final, last)
    return xt.reshape(batch, seq, d)


def _final_norm_kernel(x_ref, gn_ref, o_ref):
    o_ref[...] = _rms(x_ref[...], gn_ref[...])


def _final_norm(x, gn):
    t, d = x.shape
    tm = min(TM_FFN, t)
    return pl.pallas_call(
        _final_norm_kernel,
        out_shape=jax.ShapeDtypeStruct((t, d), F32),
        grid=(t // tm,),
        in_specs=[pl.BlockSpec((tm, d), lambda i: (i, 0)), pl.BlockSpec((1, d), lambda i: (0, 0))],
        out_specs=pl.BlockSpec((tm, d), lambda i: (i, 0)),
        compiler_params=_cparams("parallel"),
        name="final_norm",
    )(x, gn.reshape(1, d))
```
